```python
import jax, jax.numpy as jnp
from jax import lax
import numpy as np

D_MODEL = 1024
BATCH = 2
SEQ = 8192
DEPTH = 4

GRID_W = 64
CTX_LEN = 256
A_GROUPS = 4
A_GROUP_DIM = 128
A_WIDTH = A_GROUPS * A_GROUP_DIM
B_WIDTH = 512
CONV_WIDTH = 31
CONV_PAD = (CONV_WIDTH - 1) // 2
AB_IN = A_WIDTH + 2 * B_WIDTH
AB_MIX = A_WIDTH + B_WIDTH
C_HEADS = 4
C_KV_HEADS = 2
C_GROUP = C_HEADS // C_KV_HEADS
C_HEAD_DIM = 128
C_WIDTH = C_HEADS * C_HEAD_DIM
C_KV_WIDTH = C_KV_HEADS * C_HEAD_DIM
D_HEADS = 4
D_NOPE = 128
D_ROPE = 64
D_V = 128
Q_LORA = 384
KV_LORA = 256
D_WIDTH = D_HEADS * D_V
CD_IN = C_WIDTH + 2 * C_KV_WIDTH + Q_LORA + KV_LORA + D_ROPE
CD_SPLITS = [C_WIDTH, C_WIDTH + C_KV_WIDTH, C_WIDTH + 2 * C_KV_WIDTH,
             C_WIDTH + 2 * C_KV_WIDTH + Q_LORA, C_WIDTH + 2 * C_KV_WIDTH + Q_LORA + KV_LORA]
CD_MIX = C_WIDTH + D_WIDTH
Q_BLOCK = 128
ROPE_THETA = 10000.0
N_GROUPS = 4
EXPERTS_PER_GROUP = 8
N_EXPERTS = N_GROUPS * EXPERTS_PER_GROUP
TOP_K = 2
EXPERT_FF = 512
EXPERT_BLOCK = 128
N_MOD = 6
RMS_EPS = 1e-6
MOD_INIT = 0.5
ROUTER_BIAS_INIT = 0.01
N_EVEN = (DEPTH + 1) // 2
N_ODD = DEPTH // 2

kernel_name = 'hybrid_fourier_conv_gqa_mla_hmoe_dit'


def rmsnorm(x, g):
    xf = x.astype(jnp.float32)
    y = xf * lax.rsqrt(jnp.mean(xf * xf, axis=-1, keepdims=True) + RMS_EPS)
    return y.astype(x.dtype) * g


def axial_rope_tables(row, col, dim):
    n_freq = dim // 4
    inv = ROPE_THETA ** (-jnp.arange(n_freq, dtype=jnp.float32) / n_freq)
    ang = jnp.concatenate([row[:, None] * inv[None, :], col[:, None] * inv[None, :]], axis=-1)
    return jnp.cos(ang), jnp.sin(ang)


def apply_rope(x, cos, sin):
    xf = x.astype(jnp.float32).reshape(x.shape[:-1] + (x.shape[-1] // 2, 2))
    xr, xi = xf[..., 0], xf[..., 1]
    c = cos[None, :, None, :]
    s = sin[None, :, None, :]
    out = jnp.stack([xr * c - xi * s, xr * s + xi * c], axis=-1)
    return out.reshape(x.shape).astype(x.dtype)


def block_attention(q, k, v, scale):
    b, s, kh, g, d = q.shape
    nb = s // Q_BLOCK
    qb = jnp.moveaxis(q.reshape(b, nb, Q_BLOCK, kh, g, d), 1, 0)

    def one_block(qblk):
        sc = jnp.einsum('bqhgd,bkhd->bhgqk', qblk, k).astype(jnp.float32) * scale
        p = jax.nn.softmax(sc, axis=-1).astype(v.dtype)
        return jnp.einsum('bhgqk,bkhe->bqhge', p, v)

    o = lax.map(one_block, qb)
    return jnp.moveaxis(o, 0, 1).reshape(b, s, kh, g, v.shape[-1])


def fourier_conv_mixer(h, w_in, conv_w, conv_b, conv_g, w_out):
    b, L, _ = h.shape
    p = h @ w_in
    pa, pu, pg = jnp.split(p, [A_WIDTH, A_WIDTH + B_WIDTH], axis=-1)
    fa = pa.astype(jnp.float32).reshape(b, L, A_GROUPS, A_GROUP_DIM)
    ya = jnp.fft.fft2(fa, axes=(1, 3), norm='ortho').real.reshape(b, L, A_WIDTH).astype(h.dtype)
    u = pu * jax.nn.sigmoid(pg)
    u = lax.conv_general_dilated(u, conv_w[:, None, :], (1,), [(CONV_PAD, CONV_PAD)],
                                 dimension_numbers=('NWC', 'WIO', 'NWC'),
                                 feature_group_count=B_WIDTH) + conv_b
    yb = jax.nn.silu(rmsnorm(u, conv_g))
    return jnp.concatenate([ya, yb], axis=-1) @ w_out


def cd_project(t, w_in, q_g, k_g, cq_g, ckv_g, w_uq, w_ukv):
    bt, L, _ = t.shape
    q, k, v, cq, ckv, kr = jnp.split(t @ w_in, CD_SPLITS, axis=-1)
    q = rmsnorm(q.reshape(bt, L, C_HEADS, C_HEAD_DIM), q_g)
    k = rmsnorm(k.reshape(bt, L, C_KV_HEADS, C_HEAD_DIM), k_g)
    v = v.reshape(bt, L, C_KV_HEADS, C_HEAD_DIM)
    qd = (rmsnorm(cq, cq_g) @ w_uq).reshape(bt, L, D_HEADS, D_NOPE + D_ROPE)
    kvd = (rmsnorm(ckv, ckv_g) @ w_ukv).reshape(bt, L, D_HEADS, D_NOPE + D_V)
    kd_rope = kr[:, :, None, :]
    return (q, k, v, qd[..., :D_NOPE], qd[..., D_NOPE:], kvd[..., :D_NOPE], kd_rope, kvd[..., D_NOPE:])


def cd_attend(q, qdn, qdr, k, v, kdn, kdr, vd):
    b, L = q.shape[:2]
    oc = block_attention(q.reshape(b, L, C_KV_HEADS, C_GROUP, C_HEAD_DIM), k, v, C_HEAD_DIM ** -0.5)
    qd = jnp.concatenate([qdn, qdr], axis=-1)[:, :, :, None, :]
    kd = jnp.concatenate([kdn, jnp.broadcast_to(kdr, kdn.shape[:-1] + (D_ROPE,))], axis=-1)
    od = block_attention(qd, kd, vd, (D_NOPE + D_ROPE) ** -0.5)
    return jnp.concatenate([oc.reshape(b, L, C_WIDTH), od.reshape(b, L, D_WIDTH)], axis=-1)


def attn_mla_mixer(h, hc, w_in, q_g, k_g, cq_g, ckv_g, w_uq, w_ukv, w_out, rope_c, rope_d, with_ctx_out):
    q, k, v, qdn, qdr, kdn, kdr, vd = cd_project(h, w_in, q_g, k_g, cq_g, ckv_g, w_uq, w_ukv)
    cq, ck, cv, cqdn, cqdr, ckdn, ckdr, cvd = cd_project(hc, w_in, q_g, k_g, cq_g, ckv_g, w_uq, w_ukv)
    q, k = apply_rope(q, *rope_c), apply_rope(k, *rope_c)
    qdr, kdr = apply_rope(qdr, *rope_d), apply_rope(kdr, *rope_d)
    cat = lambda a, b: jnp.concatenate([a, b], axis=1)
    y = cd_attend(q, qdn, qdr, cat(ck, k), cat(cv, v), cat(ckdn, kdn), cat(ckdr, kdr), cat(cvd, vd)) @ w_out
    yc = cd_attend(cq, cqdn, cqdr, ck, cv, ckdn, ckdr, cvd) @ w_out if with_ctx_out else None
    return y, yc


def routed_experts(h, e_idx, e_w, w_gate, w_up, w_down):
    n_tok, d = h.shape
    n_exp = w_gate.shape[0]
    n_asg = e_idx.shape[0] * e_idx.shape[1]
    n_blocks = (n_asg + n_exp * (EXPERT_BLOCK - 1) + EXPERT_BLOCK - 1) // EXPERT_BLOCK
    n_rows = n_blocks * EXPERT_BLOCK
    flat_e = e_idx.reshape(-1)
    flat_tok = jnp.arange(n_asg, dtype=jnp.int32) // e_idx.shape[1]
    flat_w = e_w.reshape(-1)
    order = jnp.argsort(flat_e)
    sorted_e = flat_e[order]
    counts = jnp.bincount(flat_e, length=n_exp)
    padded = (counts + EXPERT_BLOCK - 1) // EXPERT_BLOCK * EXPERT_BLOCK
    padded_end = jnp.cumsum(padded)
    start = jnp.cumsum(counts) - counts
    dest = (padded_end - padded)[sorted_e] + jnp.arange(n_asg, dtype=jnp.int32) - start[sorted_e]
    row_tok = jnp.full((n_rows,), n_tok, jnp.int32).at[dest].set(flat_tok[order])
    row_w = jnp.zeros((n_rows,), h.dtype).at[dest].set(flat_w[order])
    block_exp = jnp.minimum(jnp.searchsorted(padded_end, jnp.arange(n_blocks, dtype=jnp.int32) * EXPERT_BLOCK,
                                             side='right'), n_exp - 1)
    h_pad = jnp.concatenate([h, jnp.zeros((1, d), h.dtype)], axis=0)
    xb = h_pad[row_tok].reshape(n_blocks, EXPERT_BLOCK, d)

    def expert_block(args):
        xblk, e = args
        return (jax.nn.silu(xblk @ w_gate[e]) * (xblk @ w_up[e])) @ w_down[e]

    yb = lax.map(expert_block, (xb, block_exp)).reshape(n_rows, d)
    return jnp.zeros((n_tok + 1, d), h.dtype).at[row_tok].add(yb * row_w[:, None])[:n_tok]


def hier_moe(h, grp_w, grp_b, exp_w, exp_b, w_gate, w_up, w_down):
    n = h.shape[0]
    g_logit = (h @ grp_w).astype(jnp.float32) + grp_b.astype(jnp.float32)
    g_idx = jnp.argmax(g_logit, axis=-1).astype(jnp.int32)
    g_prob = jnp.take_along_axis(jax.nn.softmax(g_logit, axis=-1), g_idx[:, None], axis=-1)
    e_logit = ((h @ exp_w).astype(jnp.float32) + exp_b.astype(jnp.float32)).reshape(n, N_GROUPS, EXPERTS_PER_GROUP)
    e_logit = jnp.take_along_axis(e_logit, g_idx[:, None, None], axis=1)[:, 0]
    top_v, top_i = lax.top_k(e_logit, TOP_K)
    w = (g_prob * jax.nn.softmax(top_v, axis=-1)).astype(h.dtype)
    e_idx = g_idx[:, None] * EXPERTS_PER_GROUP + top_i.astype(jnp.int32)
    return routed_experts(h, e_idx, w, w_gate, w_up, w_down)


def setup_inputs(seed: int = 0) -> dict:
    key = jax.random.key(seed)
    ks = jax.random.split(key, 29)
    f32 = jnp.float32
    nrm = lambda k, shape, s: jax.random.normal(k, shape, f32) * s
    gain = lambda k, shape: 1.0 + 0.05 * jax.random.normal(k, shape, f32)
    ne, no = N_EVEN, N_ODD
    return {
        'x': nrm(ks[0], (BATCH, SEQ, D_MODEL), 1.0),
        'c': nrm(ks[1], (BATCH, D_MODEL), 1.0),
        'ctx': nrm(ks[2], (BATCH, CTX_LEN, D_MODEL), 1.0),
        'c_ctx': nrm(ks[3], (D_MODEL,), 1.0),
        'mod_w': nrm(ks[4], (DEPTH, D_MODEL, N_MOD * D_MODEL), MOD_INIT * D_MODEL ** -0.5),
        'mod_b': nrm(ks[5], (DEPTH, N_MOD * D_MODEL), 0.02),
        'norm_mix_g': gain(ks[6], (DEPTH, D_MODEL)),
        'norm_ffn_g': gain(ks[7], (DEPTH, D_MODEL)),
        'w_in_ab': nrm(ks[8], (ne, D_MODEL, AB_IN), D_MODEL ** -0.5),
        'conv_w': nrm(ks[9], (ne, CONV_WIDTH, B_WIDTH), CONV_WIDTH ** -0.5),
        'conv_b': nrm(ks[10], (ne, B_WIDTH), 0.02),
        'conv_norm_g': gain(ks[11], (ne, B_WIDTH)),
        'w_out_ab': nrm(ks[12], (ne, AB_MIX, D_MODEL), AB_MIX ** -0.5),
        'w_in_cd': nrm(ks[13], (no, D_MODEL, CD_IN), D_MODEL ** -0.5),
        'q_norm_g': gain(ks[14], (no, C_HEAD_DIM)),
        'k_norm_g': gain(ks[15], (no, C_HEAD_DIM)),
        'cq_norm_g': gain(ks[16], (no, Q_LORA)),
        'ckv_norm_g': gain(ks[17], (no, KV_LORA)),
        'w_uq': nrm(ks[18], (no, Q_LORA, D_HEADS * (D_NOPE + D_ROPE)), Q_LORA ** -0.5),
        'w_ukv': nrm(ks[19], (no, KV_LORA, D_HEADS * (D_NOPE + D_V)), KV_LORA ** -0.5),
        'w_out_cd': nrm(ks[20], (no, CD_MIX, D_MODEL), CD_MIX ** -0.5),
        'router_grp_w': nrm(ks[21], (DEPTH, D_MODEL, N_GROUPS), D_MODEL ** -0.5),
        'router_grp_b': nrm(ks[22], (DEPTH, N_GROUPS), ROUTER_BIAS_INIT),
        'router_exp_w': nrm(ks[23], (DEPTH, D_MODEL, N_EXPERTS), D_MODEL ** -0.5),
        'router_exp_b': nrm(ks[24], (DEPTH, N_EXPERTS), ROUTER_BIAS_INIT),
        'exp_w_gate': nrm(ks[25], (DEPTH, N_EXPERTS, D_MODEL, EXPERT_FF), D_MODEL ** -0.5),
        'exp_w_up': nrm(ks[26], (DEPTH, N_EXPERTS, D_MODEL, EXPERT_FF), D_MODEL ** -0.5),
        'exp_w_down': nrm(ks[27], (DEPTH, N_EXPERTS, EXPERT_FF, D_MODEL), EXPERT_FF ** -0.5),
        'final_norm_g': gain(ks[28], (D_MODEL,)),
    }


def reference(x, c, ctx, c_ctx, mod_w, mod_b, norm_mix_g, norm_ffn_g, w_in_ab, conv_w, conv_b,
              conv_norm_g, w_out_ab, w_in_cd, q_norm_g, k_norm_g, cq_norm_g, ckv_norm_g, w_uq, w_ukv,
              w_out_cd, router_grp_w, router_grp_b, router_exp_w, router_exp_b, exp_w_gate, exp_w_up,
              exp_w_down, final_norm_g):
    b, s, d = x.shape
    rows = s // GRID_W
    row_pos = jnp.repeat(jnp.arange(rows, dtype=jnp.float32), GRID_W)
    col_pos = jnp.tile(jnp.arange(GRID_W, dtype=jnp.float32), rows)
    rope_c = axial_rope_tables(row_pos, col_pos, C_HEAD_DIM)
    rope_d = axial_rope_tables(row_pos, col_pos, D_ROPE)
    silu_c = jax.nn.silu(c)
    silu_cc = jax.nn.silu(c_ctx)
    xc = ctx
    for layer in range(DEPTH):
        last = layer == DEPTH - 1
        j = layer // 2
        sh1, sc1, g1, sh2, sc2, g2 = jnp.split((silu_c @ mod_w[layer] + mod_b[layer])[:, None, :], N_MOD, axis=-1)
        csh1, csc1, cg1, csh2, csc2, cg2 = jnp.split(silu_cc @ mod_w[layer] + mod_b[layer], N_MOD)
        h = rmsnorm(x, norm_mix_g[layer]) * (1.0 + sc1) + sh1
        if layer % 2 == 0:
            ab = (w_in_ab[j], conv_w[j], conv_b[j], conv_norm_g[j], w_out_ab[j])
            x = x + g1 * fourier_conv_mixer(h, *ab)
            if not last:
                hc = rmsnorm(xc, norm_mix_g[layer]) * (1.0 + csc1) + csh1
                xc = xc + cg1 * fourier_conv_mixer(hc, *ab)
        else:
            hc = rmsnorm(xc, norm_mix_g[layer]) * (1.0 + csc1) + csh1
            y, yc = attn_mla_mixer(h, hc, w_in_cd[j], q_norm_g[j], k_norm_g[j], cq_norm_g[j], ckv_norm_g[j],
                                   w_uq[j], w_ukv[j], w_out_cd[j], rope_c, rope_d, not last)
            x = x + g1 * y
            if not last:
                xc = xc + cg1 * yc
        moe = (router_grp_w[layer], router_grp_b[layer], router_exp_w[layer], router_exp_b[layer],
               exp_w_gate[layer], exp_w_up[layer], exp_w_down[layer])
        h2 = rmsnorm(x, norm_ffn_g[layer]) * (1.0 + sc2) + sh2
        if last:
            x = x + g2 * hier_moe(h2.reshape(b * s, d), *moe).reshape(x.shape)
        else:
            hc2 = rmsnorm(xc, norm_ffn_g[layer]) * (1.0 + csc2) + csh2
            out = hier_moe(jnp.concatenate([h2.reshape(b * s, d), hc2.reshape(-1, d)], axis=0), *moe)
            x = x + g2 * out[: b * s].reshape(x.shape)
            xc = xc + cg2 * out[b * s:].reshape(xc.shape)
    return rmsnorm(x, final_norm_g)
```

```python
import functools
import math

import numpy as np
import jax
import jax.numpy as jnp
from jax import lax
from jax.experimental import pallas as pl
from jax.experimental.pallas import tpu as pltpu

F32 = jnp.float32
BF16 = jnp.bfloat16

D_MODEL = 1024
GRID_W = 64
A_GROUPS = 4
A_GROUP_DIM = 128
A_WIDTH = 512
B_WIDTH = 512
CONV_WIDTH = 31
CONV_PAD = 15
C_HEADS = 4
C_KV_HEADS = 2
C_HEAD_DIM = 128
C_WIDTH = 512
C_KV_WIDTH = 256
D_HEADS = 4
D_NOPE = 128
D_ROPE = 64
D_QK = D_NOPE + D_ROPE
D_V = 128
Q_LORA = 384
KV_LORA = 256
D_WIDTH = 512
N_GROUPS = 4
EXPERTS_PER_GROUP = 8
N_EXPERTS = 32
EXPERT_FF = 512
N_MOD = 6
RMS_EPS = 1e-6
ROPE_THETA = 10000.0

LANES = 128
SUBLANES = 8
ROW_TILE = 256
HALO = 16
FOURIER_N2 = 128
EXPERT_TILE = 256
VMEM_LIMIT = 56 * 1024 * 1024


def _cparams(sem):
    return pltpu.CompilerParams(dimension_semantics=sem, vmem_limit_bytes=VMEM_LIMIT)


def _split3(x):
    hi = x.astype(BF16)
    r1 = x - hi.astype(F32)
    mid = r1.astype(BF16)
    lo = (r1 - mid.astype(F32)).astype(BF16)
    return hi, mid, lo


def _dot(a, b):
    return jnp.dot(a, b, preferred_element_type=F32)


def _dot_nt(a, b):
    return lax.dot_general(a, b, (((1,), (1,)), ((), ())), preferred_element_type=F32)


def _dot_precise(a, b):
    a0, a1, a2 = _split3(a)
    b0, b1, b2 = _split3(b)
    out = _dot(a0, b0)
    out += _dot(a0, b1) + _dot(a1, b0)
    out += _dot(a0, b2) + _dot(a1, b1) + _dot(a2, b0)
    return out


def _dot_nt_precise(a, b):
    a0, a1, a2 = _split3(a)
    b0, b1, b2 = _split3(b)
    out = _dot_nt(a0, b0)
    out += _dot_nt(a0, b1) + _dot_nt(a1, b0)
    out += _dot_nt(a0, b2) + _dot_nt(a1, b1) + _dot_nt(a2, b0)
    return out


def _silu(x):
    return x * (1.0 / (1.0 + jnp.exp(-x)))


def _rms(x):
    return x * lax.rsqrt(jnp.mean(x * x, axis=-1, keepdims=True) + RMS_EPS)


def _mods_kernel(c_ref, w_ref, b_ref, o_ref):
    s = _silu(c_ref[...])
    o_ref[0] = _dot_precise(s, w_ref[0]) + b_ref[0]


def compute_mods(cvec, mod_w, mod_b):
    depth, d, n = mod_w.shape
    tn = 1024
    return pl.pallas_call(
        _mods_kernel,
        out_shape=jax.ShapeDtypeStruct((depth, SUBLANES, n), F32),
        grid=(depth, n // tn),
        in_specs=[
            pl.BlockSpec((SUBLANES, d), lambda l, j: (0, 0)),
            pl.BlockSpec((1, d, tn), lambda l, j: (l, 0, j)),
            pl.BlockSpec((1, 1, tn), lambda l, j: (l, 0, j)),
        ],
        out_specs=pl.BlockSpec((1, SUBLANES, tn), lambda l, j: (l, 0, j)),
        compiler_params=_cparams(("parallel", "parallel")),
        name="mods",
    )(cvec, mod_w, mod_b.reshape(depth, 1, n))


def _mod_row(i, tiles_per_batch, n_batch):
    return jnp.minimum(i // tiles_per_batch, n_batch)


def _norm_proj_kernel(splits, tiles_per_batch, n_batch, chunk, x_ref, g_ref, mod_ref, w_ref, *o_refs):
    i = pl.program_id(0)
    r = _mod_row(i, tiles_per_batch, n_batch)
    d = x_ref.shape[1]
    sh = mod_ref[pl.ds(r, 1), chunk * d:(chunk + 1) * d]
    sc = mod_ref[pl.ds(r, 1), (chunk + 1) * d:(chunk + 2) * d]
    h = _rms(x_ref[...]) * g_ref[...] * (1.0 + sc) + sh
    p = _dot(h.astype(BF16), w_ref[...])
    off = 0
    for o_ref, n in zip(o_refs, splits):
        o_ref[...] = p[:, off:off + n].astype(o_ref.dtype)
        off += n


def norm_proj(x, gain, mods, w_bf16, splits, chunk, tiles_per_batch, n_batch, tm=512):
    t, d = x.shape
    n = w_bf16.shape[1]
    assert sum(splits) == n and t % tm == 0
    kern = functools.partial(_norm_proj_kernel, tuple(splits), tiles_per_batch, n_batch, chunk)
    return pl.pallas_call(
        kern,
        out_shape=[jax.ShapeDtypeStruct((t, s), F32) for s in splits],
        grid=(t // tm,),
        in_specs=[
            pl.BlockSpec((tm, d), lambda i: (i, 0)),
            pl.BlockSpec((1, d), lambda i: (0, 0)),
            pl.BlockSpec(mods.shape, lambda i: (0, 0)),
            pl.BlockSpec((d, n), lambda i: (0, 0)),
        ],
        out_specs=[pl.BlockSpec((tm, s), lambda i: (i, 0)) for s in splits],
        compiler_params=_cparams(("parallel",)),
        name="norm_proj",
    )(x, gain.reshape(1, d), mods, w_bf16)


def _dft_tables(n1, n2, group_dim):
    ell = n1 * n2
    k1 = np.arange(n1, dtype=np.float64)
    ang1 = 2.0 * np.pi * np.outer(k1, k1) / n1
    f1 = np.concatenate([np.cos(ang1), -np.sin(ang1)], axis=0) / np.sqrt(n1)
    l2 = np.arange(n2, dtype=np.float64)
    k2 = np.arange(n2, dtype=np.float64)
    theta = 2.0 * np.pi * (k2[None, :, None] * l2[None, None, :] / n2
                           + k1[:, None, None] * l2[None, None, :] / ell)
    gr, gi = np.cos(theta), -np.sin(theta)
    g = np.concatenate([np.concatenate([gr, -gi], axis=2),
                        np.concatenate([gi, gr], axis=2)], axis=1) / np.sqrt(n2)
    c = np.arange(group_dim, dtype=np.float64)
    angc = 2.0 * np.pi * np.outer(c, c) / group_dim
    cs = np.concatenate([np.cos(angc), np.sin(angc)], axis=0) / np.sqrt(group_dim)
    return (jnp.asarray(f1, F32).astype(BF16), jnp.asarray(g, F32).astype(BF16),
            jnp.asarray(cs, F32).astype(BF16))


DFT_L2_BLOCK = 16


def _dft_stage1_kernel(f_ref, x_ref, o_ref):
    f = f_ref[...]
    for s in range(x_ref.shape[1]):
        o_ref[0, :, s, :] = _dot(f, x_ref[:, s, :].astype(BF16)).astype(o_ref.dtype)


def dft_stage1(f1, x3, n_batch, n1):
    _, n2, width = x3.shape
    sb = DFT_L2_BLOCK
    return pl.pallas_call(
        _dft_stage1_kernel,
        out_shape=jax.ShapeDtypeStruct((n_batch, 2 * n1, n2, width), BF16),
        grid=(n_batch, n2 // sb),
        in_specs=[
            pl.BlockSpec((2 * n1, n1), lambda b, j: (0, 0)),
            pl.BlockSpec((n1, sb, width), lambda b, j: (b, j, 0)),
        ],
        out_specs=pl.BlockSpec((1, 2 * n1, sb, width), lambda b, j: (b, 0, j, 0)),
        compiler_params=_cparams(("parallel", "parallel")),
        name="dft_stage1",
    )(f1, x3)


def _dft_stage2_kernel(kb, n2, real_input, *refs):
    if real_input:
        g_ref, cs_ref, yr_ref, o_ref = refs
    else:
        g_ref, cs_ref, yr_ref, yi_ref, o_ref = refs
    width = yr_ref.shape[-1]
    gd = cs_ref.shape[1]
    for k in range(kb):
        if real_input:
            z = _dot(g_ref[k, :, :n2], yr_ref[0, k].astype(BF16))
        else:
            y = jnp.concatenate([yr_ref[0, k].astype(BF16), yi_ref[0, k].astype(BF16)], axis=0)
            z = _dot(g_ref[k], y)
        zr, zi = z[:n2], z[n2:]
        for grp in range(width // gd):
            sl = slice(grp * gd, (grp + 1) * gd)
            lhs = jnp.concatenate([zr[:, sl], zi[:, sl]], axis=1).astype(BF16)
            o_ref[0, :, k, sl] = _dot(lhs, cs_ref[...]).astype(o_ref.dtype)


def dft_stage2(g, cs, y4, n1, n2, real_input, kb):
    n_batch, _, _, width = y4.shape
    kern = functools.partial(_dft_stage2_kernel, kb, n2, real_input)
    in_specs = [
        pl.BlockSpec((kb, 2 * n2, 2 * n2), lambda b, j: (j, 0, 0)),
        pl.BlockSpec(cs.shape, lambda b, j: (0, 0)),
        pl.BlockSpec((1, kb, n2, width), lambda b, j: (b, j, 0, 0)),
    ]
    args = [g, cs, y4]
    if not real_input:
        nj = n1 // kb
        in_specs.append(pl.BlockSpec((1, kb, n2, width), lambda b, j: (b, nj + j, 0, 0)))
        args.append(y4)
    return pl.pallas_call(
        kern,
        out_shape=jax.ShapeDtypeStruct((n_batch, n2, n1, width), BF16),
        grid=(n_batch, n1 // kb),
        in_specs=in_specs,
        out_specs=pl.BlockSpec((1, n2, kb, width), lambda b, j: (b, 0, j, 0)),
        compiler_params=_cparams(("parallel", "parallel")),
        name="dft_stage2",
    )(*args)


def fourier_mix(pa, n_batch, seq, ctx_len, with_ctx):
    t, width = pa.shape
    n2 = FOURIER_N2
    n1 = seq // n2
    f1, g, cs = _dft_tables(n1, n2, A_GROUP_DIM)
    y4 = dft_stage1(f1, pa.reshape(t // n2, n2, width), n_batch, n1)
    kb = min(16, n1)
    ya_lat = dft_stage2(g, cs, y4, n1, n2, False, kb).reshape(n_batch * seq, width)
    ya_ctx = None
    if with_ctx:
        _, gc, _ = _dft_tables(1, ctx_len, A_GROUP_DIM)
        xc = pa[n_batch * seq:].reshape(n_batch, 1, ctx_len, width)
        ya_ctx = dft_stage2(gc, cs, xc, 1, ctx_len, True, 1).reshape(n_batch * ctx_len, width)
    return ya_lat, ya_ctx


class _Tiles:
    def __init__(self, n_batch, seq, ctx_len, tm=ROW_TILE):
        assert seq % tm == 0 and ctx_len % tm == 0
        self.tm = tm
        self.n_batch = n_batch
        self.seq = seq
        self.ctx_len = ctx_len
        self.tps = seq // tm
        self.ctps = ctx_len // tm
        self.n_lat = n_batch * self.tps
        self.n_ctx = n_batch * self.ctps
        self.n_tiles = self.n_lat + self.n_ctx
        self.n_tok = self.n_tiles * tm

    def mod_row(self, i):
        return jnp.minimum(i // self.tps, self.n_batch)

    def lat_index(self, i):
        return jnp.minimum(i, self.n_lat - 1)

    def ctx_index(self, i):
        return jnp.maximum(i - self.n_lat, 0)

    def seq_pos(self, i):
        ic = i - self.n_lat
        is_ctx = i >= self.n_lat
        b = jnp.where(is_ctx, ic // self.ctps, i // self.tps)
        j = jnp.where(is_ctx, self.tps + ic % self.ctps, i % self.tps)
        return b, j

    def seq_first(self, i):
        ic = i - self.n_lat
        return jnp.where(i >= self.n_lat, ic % self.ctps == 0, i % self.tps == 0)

    def seq_last(self, i):
        ic = i - self.n_lat
        return jnp.where(i >= self.n_lat, ic % self.ctps == self.ctps - 1, i % self.tps == self.tps - 1)


CONV_CHUNK = 32


def _even_out_kernel(tiles, x_ref, yal_ref, yac_ref, ug_ref, prev_ref, next_ref, cw_ref, cb_ref, cg_ref,
                     w_ref, mod_ref, o_ref, ubuf, ybuf):
    i = pl.program_id(0)
    tm = tiles.tm
    bw = B_WIDTH
    d = x_ref.shape[1]

    def glu(ref):
        v = ref[...]
        return v[:, :bw] * (1.0 / (1.0 + jnp.exp(-v[:, bw:])))

    first = tiles.seq_first(i)
    last = tiles.seq_last(i)
    ubuf[0:HALO, :] = jnp.where(first, 0.0, glu(prev_ref))
    ubuf[HALO:HALO + tm, :] = glu(ug_ref)
    ubuf[HALO + tm:HALO + tm + HALO, :] = jnp.where(last, 0.0, glu(next_ref))

    ya = jnp.where(i >= tiles.n_lat, yac_ref[...], yal_ref[...])
    r = tiles.mod_row(i)
    gate = mod_ref[pl.ds(r, 1), 2 * d:3 * d]
    cb = cb_ref[...]
    cg = cg_ref[...]
    base = HALO - CONV_PAD
    for r0 in range(0, tm, CONV_CHUNK):
        acc = jnp.zeros((CONV_CHUNK, bw), F32)
        for k in range(CONV_WIDTH):
            acc = acc + cw_ref[k:k + 1, :] * ubuf[r0 + base + k:r0 + base + k + CONV_CHUNK, :]
        ybuf[r0:r0 + CONV_CHUNK, :] = _silu(_rms(acc + cb) * cg).astype(BF16)
    y = _dot(ya, w_ref[:A_WIDTH, :]) + _dot(ybuf[...], w_ref[A_WIDTH:, :])
    o_ref[...] = x_ref[...] + gate * y


def even_out(tiles, x, ya_lat, ya_ctx, pug, conv_w, conv_b, conv_g, w_out_bf16, mods):
    t, d = x.shape
    tm = tiles.tm
    n_tiles = t // tm
    hb = tm // HALO
    n_hblk = t // HALO
    if ya_ctx is None:
        ya_ctx = ya_lat
    cw = jnp.concatenate([conv_w, jnp.zeros((1, B_WIDTH), F32)], axis=0)
    kern = functools.partial(_even_out_kernel, tiles)
    return pl.pallas_call(
        kern,
        out_shape=jax.ShapeDtypeStruct((t, d), F32),
        grid=(n_tiles,),
        in_specs=[
            pl.BlockSpec((tm, d), lambda i: (i, 0)),
            pl.BlockSpec((tm, A_WIDTH), lambda i: (tiles.lat_index(i), 0)),
            pl.BlockSpec((tm, A_WIDTH), lambda i: (tiles.ctx_index(i), 0)),
            pl.BlockSpec((tm, 2 * B_WIDTH), lambda i: (i, 0)),
            pl.BlockSpec((HALO, 2 * B_WIDTH), lambda i: (jnp.maximum(i * hb - 1, 0), 0)),
            pl.BlockSpec((HALO, 2 * B_WIDTH), lambda i: (jnp.minimum((i + 1) * hb, n_hblk - 1), 0)),
            pl.BlockSpec((CONV_WIDTH + 1, B_WIDTH), lambda i: (0, 0)),
            pl.BlockSpec((1, B_WIDTH), lambda i: (0, 0)),
            pl.BlockSpec((1, B_WIDTH), lambda i: (0, 0)),
            pl.BlockSpec((d, d), lambda i: (0, 0)),
            pl.BlockSpec(mods.shape, lambda i: (0, 0)),
        ],
        out_specs=pl.BlockSpec((tm, d), lambda i: (i, 0)),
        scratch_shapes=[pltpu.VMEM((tm + 2 * HALO, B_WIDTH), F32), pltpu.VMEM((tm, B_WIDTH), BF16)],
        compiler_params=_cparams(("parallel",)),
        name="even_out",
    )(x, ya_lat, ya_ctx, pug, pug, pug, cw, conv_b.reshape(1, B_WIDTH), conv_g.reshape(1, B_WIDTH),
      w_out_bf16, mods)


def _rope_tables(seq, tm, dim):
    n_freq = dim // 4
    pos = jnp.arange(seq, dtype=jnp.int32)
    row = (pos // GRID_W).astype(F32)
    col = (pos % GRID_W).astype(F32)
    inv = ROPE_THETA ** (-jnp.arange(n_freq, dtype=F32) / n_freq)
    ang = jnp.concatenate([row[:, None] * inv[None, :], col[:, None] * inv[None, :]], axis=-1)
    cos = jnp.repeat(jnp.cos(ang), 2, axis=-1)
    sin = jnp.repeat(jnp.sin(ang), 2, axis=-1)
    sign = jnp.tile(jnp.asarray([-1.0, 1.0], F32), dim // 2)
    cos = jnp.concatenate([cos, jnp.ones((tm, dim), F32)], axis=0)
    sin = jnp.concatenate([sin * sign, jnp.zeros((tm, dim), F32)], axis=0)
    return cos, sin


def _pair_swap(x):
    n = x.shape[1]
    lane = lax.broadcasted_iota(jnp.int32, x.shape, 1)
    nxt = pltpu.roll(x, n - 1, 1)
    prv = pltpu.roll(x, 1, 1)
    return jnp.where((lane & 1) == 0, nxt, prv)


def _rope(x, cos, sin):
    parts = []
    for a in range(x.shape[1] // LANES):
        xa = x[:, a * LANES:(a + 1) * LANES]
        parts.append(xa * cos + _pair_swap(xa) * sin)
    return parts[0] if len(parts) == 1 else jnp.concatenate(parts, axis=1)


def _attn_prep_kernel(p_ref, cosc_ref, sinc_ref, cosd_ref, sind_ref, qg_ref, kg_ref, cqg_ref, ckvg_ref,
                      wuq_ref, wukv_ref, qc_ref, kc_ref, vct_ref, qd_ref, kd_ref, vdt_ref):
    hd = C_HEAD_DIM
    p = p_ref[...]
    o_k = C_WIDTH
    o_v = o_k + C_KV_WIDTH
    o_cq = o_v + C_KV_WIDTH
    o_ckv = o_cq + Q_LORA
    o_kr = o_ckv + KV_LORA
    cosc, sinc = cosc_ref[...], sinc_ref[...]
    for h in range(C_HEADS):
        qh = _rms(p[:, h * hd:(h + 1) * hd]) * qg_ref[...]
        qc_ref[h] = (_rope(qh, cosc, sinc) * (hd ** -0.5)).astype(qc_ref.dtype)
    for h in range(C_KV_HEADS):
        kh = _rms(p[:, o_k + h * hd:o_k + (h + 1) * hd]) * kg_ref[...]
        kc_ref[h] = _rope(kh, cosc, sinc).astype(kc_ref.dtype)
        vct_ref[h] = p[:, o_v + h * hd:o_v + (h + 1) * hd].T.astype(vct_ref.dtype)
    cosd = jnp.concatenate([cosd_ref[...], cosd_ref[...]], axis=1)
    sind = jnp.concatenate([sind_ref[...], sind_ref[...]], axis=1)
    cq = (_rms(p[:, o_cq:o_cq + Q_LORA]) * cqg_ref[...]).astype(BF16)
    qd = _dot(cq, wuq_ref[...]) * (D_QK ** -0.5)
    qr = _rope(qd[:, D_HEADS * D_NOPE:], cosd, sind)
    ckv = (_rms(p[:, o_ckv:o_ckv + KV_LORA]) * ckvg_ref[...]).astype(BF16)
    kvd = _dot(ckv, wukv_ref[...])
    kr2 = jnp.concatenate([p[:, o_kr:o_kr + D_ROPE], p[:, o_kr:o_kr + D_ROPE]], axis=1)
    kr = _rope(kr2, cosd, sind)[:, :D_ROPE]
    for h in range(D_HEADS):
        qd_ref[h, :, :D_NOPE] = qd[:, h * D_NOPE:(h + 1) * D_NOPE].astype(qd_ref.dtype)
        qd_ref[h, :, D_NOPE:] = qr[:, h * D_ROPE:(h + 1) * D_ROPE].astype(qd_ref.dtype)
        kd_ref[h, :, :D_NOPE] = kvd[:, h * 2 * D_NOPE:h * 2 * D_NOPE + D_NOPE].astype(kd_ref.dtype)
        kd_ref[h, :, D_NOPE:] = kr.astype(kd_ref.dtype)
        vdt_ref[h] = kvd[:, h * 2 * D_NOPE + D_NOPE:(h + 1) * 2 * D_NOPE].T.astype(vdt_ref.dtype)


def attn_prep(tiles, p, q_g, k_g, cq_g, ckv_g, w_uq_bf16, w_ukv_bf16):
    tm = tiles.tm
    nb = tiles.n_batch
    stot = tiles.seq + tiles.ctx_len
    cosc, sinc = _rope_tables(tiles.seq, tm, C_HEAD_DIM)
    cosd, sind = _rope_tables(tiles.seq, tm, D_ROPE)

    def rope_idx(i):
        return (jnp.where(i >= tiles.n_lat, tiles.tps, i % tiles.tps), 0)

    def rows_idx(i):
        b, j = tiles.seq_pos(i)
        return (b, 0, j, 0)

    def cols_idx(i):
        b, j = tiles.seq_pos(i)
        return (b, 0, 0, j)

    const = lambda i: (0, 0)
    out_shape = [
        jax.ShapeDtypeStruct((nb, C_HEADS, stot, C_HEAD_DIM), BF16),
        jax.ShapeDtypeStruct((nb, C_KV_HEADS, stot, C_HEAD_DIM), BF16),
        jax.ShapeDtypeStruct((nb, C_KV_HEADS, C_HEAD_DIM, stot), BF16),
        jax.ShapeDtypeStruct((nb, D_HEADS, stot, D_QK), BF16),
        jax.ShapeDtypeStruct((nb, D_HEADS, stot, D_QK), BF16),
        jax.ShapeDtypeStruct((nb, D_HEADS, D_V, stot), BF16),
    ]
    out_specs = [
        pl.BlockSpec((None, C_HEADS, tm, C_HEAD_DIM), rows_idx),
        pl.BlockSpec((None, C_KV_HEADS, tm, C_HEAD_DIM), rows_idx),
        pl.BlockSpec((None, C_KV_HEADS, C_HEAD_DIM, tm), cols_idx),
        pl.BlockSpec((None, D_HEADS, tm, D_QK), rows_idx),
        pl.BlockSpec((None, D_HEADS, tm, D_QK), rows_idx),
        pl.BlockSpec((None, D_HEADS, D_V, tm), cols_idx),
    ]
    return pl.pallas_call(
        _attn_prep_kernel,
        out_shape=out_shape,
        grid=(tiles.n_tiles,),
        in_specs=[
            pl.BlockSpec((tm, p.shape[1]), lambda i: (i, 0)),
            pl.BlockSpec((tm, C_HEAD_DIM), rope_idx),
            pl.BlockSpec((tm, C_HEAD_DIM), rope_idx),
            pl.BlockSpec((tm, D_ROPE), rope_idx),
            pl.BlockSpec((tm, D_ROPE), rope_idx),
            pl.BlockSpec((1, C_HEAD_DIM), const),
            pl.BlockSpec((1, C_HEAD_DIM), const),
            pl.BlockSpec((1, Q_LORA), const),
            pl.BlockSpec((1, KV_LORA), const),
            pl.BlockSpec(w_uq_bf16.shape, const),
            pl.BlockSpec(w_ukv_bf16.shape, const),
        ],
        out_specs=out_specs,
        compiler_params=_cparams(("parallel",)),
        name="attn_prep",
    )(p, cosc, sinc, cosd, sind, q_g.reshape(1, -1), k_g.reshape(1, -1), cq_g.reshape(1, -1),
      ckv_g.reshape(1, -1), w_uq_bf16, w_ukv_bf16)


def _permute_w_uq(w_uq):
    w = w_uq.reshape(Q_LORA, D_HEADS, D_QK)
    return jnp.concatenate([w[:, :, :D_NOPE].reshape(Q_LORA, D_HEADS * D_NOPE),
                            w[:, :, D_NOPE:].reshape(Q_LORA, D_HEADS * D_ROPE)], axis=1)


def _flash_kernel(q_ref, k_ref, vt_ref, o_ref, m_ref, l_ref, acc_ref):
    ki = pl.program_id(3)

    @pl.when(ki == 0)
    def _():
        m_ref[...] = jnp.full(m_ref.shape, -jnp.inf, F32)
        l_ref[...] = jnp.zeros(l_ref.shape, F32)
        acc_ref[...] = jnp.zeros(acc_ref.shape, F32)

    s = _dot_nt(k_ref[...], q_ref[...])
    m_prev = m_ref[...]
    m_new = jnp.maximum(m_prev, jnp.max(s, axis=0, keepdims=True))
    alpha = jnp.exp(m_prev - m_new)
    pr = jnp.exp(s - m_new)
    l_ref[...] = alpha * l_ref[...] + jnp.sum(pr, axis=0, keepdims=True)
    acc_ref[...] = alpha * acc_ref[...] + _dot(vt_ref[...], pr.astype(BF16))
    m_ref[...] = m_new

    @pl.when(ki == pl.num_programs(3) - 1)
    def _():
        o_ref[...] = (acc_ref[...] * (1.0 / l_ref[...])).T.astype(o_ref.dtype)


def _pick_tile(n, cap, step=ROW_TILE):
    best = step
    for t in range(step, cap + 1, step):
        if n % t == 0:
            best = t
    return best


def flash_attention(q, k, vt, sq, sk, q_off, k_off, tq_cap=512, tk_cap=1024):
    nb, nh, _, dqk = q.shape
    nkh, dv = vt.shape[1], vt.shape[2]
    grp = nh // nkh
    tq = _pick_tile(math.gcd(sq, q_off) if q_off else sq, tq_cap)
    tk = _pick_tile(math.gcd(sk, k_off) if k_off else sk, tk_cap)
    qo, ko = q_off // tq, k_off // tk
    return pl.pallas_call(
        _flash_kernel,
        out_shape=jax.ShapeDtypeStruct((nb, sq, nh * dv), BF16),
        grid=(nb, nh, sq // tq, sk // tk),
        in_specs=[
            pl.BlockSpec((None, None, tq, dqk), lambda b, h, qi, ki: (b, h, qo + qi, 0)),
            pl.BlockSpec((None, None, tk, dqk), lambda b, h, qi, ki: (b, h // grp, ko + ki, 0)),
            pl.BlockSpec((None, None, dv, tk), lambda b, h, qi, ki: (b, h // grp, 0, ko + ki)),
        ],
        out_specs=pl.BlockSpec((None, tq, dv), lambda b, h, qi, ki: (b, qi, h)),
        scratch_shapes=[pltpu.VMEM((1, tq), F32), pltpu.VMEM((1, tq), F32), pltpu.VMEM((dv, tq), F32)],
        compiler_params=_cparams(("parallel", "parallel", "parallel", "arbitrary")),
        name="flash_attention",
    )(q, k, vt)


def _attn_out_kernel(tiles, with_ctx, x_ref, *refs):
    if with_ctx:
        ocl_ref, odl_ref, occ_ref, odc_ref, w_ref, mod_ref, o_ref = refs
    else:
        ocl_ref, odl_ref, w_ref, mod_ref, o_ref = refs
    i = pl.program_id(0)
    d = x_ref.shape[1]
    oc, od = ocl_ref[...], odl_ref[...]
    if with_ctx:
        is_ctx = i >= tiles.n_lat
        oc = jnp.where(is_ctx, occ_ref[...], oc)
        od = jnp.where(is_ctx, odc_ref[...], od)
    gate = mod_ref[pl.ds(tiles.mod_row(i), 1), 2 * d:3 * d]
    y = _dot(oc, w_ref[:C_WIDTH, :]) + _dot(od, w_ref[C_WIDTH:, :])
    o_ref[...] = x_ref[...] + gate * y


def attn_out(tiles, x, oc_lat, od_lat, oc_ctx, od_ctx, w_out_bf16, mods):
    t, d = x.shape
    tm = tiles.tm
    with_ctx = oc_ctx is not None
    n_out_tiles = tiles.n_tiles if with_ctx else tiles.n_lat
    lat_spec = pl.BlockSpec((tm, C_WIDTH), lambda i: (tiles.lat_index(i), 0))
    ctx_spec = pl.BlockSpec((tm, C_WIDTH), lambda i: (tiles.ctx_index(i), 0))
    in_specs = [pl.BlockSpec((tm, d), lambda i: (i, 0)), lat_spec, lat_spec]
    args = [x, oc_lat, od_lat]
    if with_ctx:
        in_specs += [ctx_spec, ctx_spec]
        args += [oc_ctx, od_ctx]
    in_specs += [pl.BlockSpec((d, d), lambda i: (0, 0)), pl.BlockSpec(mods.shape, lambda i: (0, 0))]
    args += [w_out_bf16, mods]
    return pl.pallas_call(
        functools.partial(_attn_out_kernel, tiles, with_ctx),
        out_shape=jax.ShapeDtypeStruct((n_out_tiles * tm, d), F32),
        grid=(n_out_tiles,),
        in_specs=in_specs,
        out_specs=pl.BlockSpec((tm, d), lambda i: (i, 0)),
        compiler_params=_cparams(("parallel",)),
        name="attn_out",
    )(*args)


def attn_layer(tiles, xs, gain, mods, w_in_bf16, q_g, k_g, cq_g, ckv_g, w_uq_bf16, w_ukv_bf16, w_out_bf16, with_ctx_out):
    nb, seq, ctx_len = tiles.n_batch, tiles.seq, tiles.ctx_len
    (p,) = norm_proj(xs, gain, mods, w_in_bf16, (w_in_bf16.shape[1],), 0, seq // 512, nb)
    qc, kc, vct, qd, kd, vdt = attn_prep(tiles, p, q_g, k_g, cq_g, ckv_g, w_uq_bf16, w_ukv_bf16)
    stot = seq + ctx_len
    oc_lat = flash_attention(qc, kc, vct, seq, stot, 0, 0).reshape(nb * seq, C_WIDTH)
    od_lat = flash_attention(qd, kd, vdt, seq, stot, 0, 0).reshape(nb * seq, D_WIDTH)
    oc_ctx = od_ctx = None
    if with_ctx_out:
        oc_ctx = flash_attention(qc, kc, vct, ctx_len, ctx_len, seq, seq).reshape(nb * ctx_len, C_WIDTH)
        od_ctx = flash_attention(qd, kd, vdt, ctx_len, ctx_len, seq, seq).reshape(nb * ctx_len, D_WIDTH)
    return attn_out(tiles, xs, oc_lat, od_lat, oc_ctx, od_ctx, w_out_bf16, mods)


ROUTER_ROWS = 40
NEG_BIG = -1e30


def _to_token_tiles(ref, val):
    tm = val.shape[0]
    for s in range(SUBLANES):
        ref[pl.ds(s, tm, stride=SUBLANES), :] = val[:, s * LANES:(s + 1) * LANES]


def _from_token_tiles(ref, tm):
    return jnp.concatenate([ref[pl.ds(s, tm, stride=SUBLANES), :] for s in range(SUBLANES)], axis=1)


def _router_kernel(tiles, x_ref, g_ref, mod_ref, wr_ref, br_ref, h3_ref, eidx_ref, wt_ref):
    i = pl.program_id(0)
    d = x_ref.shape[1]
    tm = x_ref.shape[0]
    r = tiles.mod_row(i)
    sh = mod_ref[pl.ds(r, 1), 3 * d:4 * d]
    sc = mod_ref[pl.ds(r, 1), 4 * d:5 * d]
    h2 = _rms(x_ref[...]) * g_ref[...] * (1.0 + sc) + sh
    _to_token_tiles(h3_ref, h2)
    logits = _dot_nt_precise(wr_ref[...], h2) + br_ref[...]
    sub = lax.broadcasted_iota(jnp.int32, (SUBLANES, tm), 0)
    gl = logits[0:SUBLANES]
    gmax = jnp.max(gl, axis=0, keepdims=True)
    g_idx = jnp.min(jnp.where(gl == gmax, sub, SUBLANES), axis=0, keepdims=True)
    g_prob = 1.0 / jnp.sum(jnp.exp(gl - gmax), axis=0, keepdims=True)
    el = jnp.zeros((SUBLANES, tm), F32)
    for g in range(N_GROUPS):
        el = jnp.where(g_idx == g, logits[SUBLANES * (g + 1):SUBLANES * (g + 2)], el)
    v1 = jnp.max(el, axis=0, keepdims=True)
    i1 = jnp.min(jnp.where(el == v1, sub, SUBLANES), axis=0, keepdims=True)
    el2 = jnp.where(sub == i1, NEG_BIG, el)
    v2 = jnp.max(el2, axis=0, keepdims=True)
    i2 = jnp.min(jnp.where(el2 == v2, sub, SUBLANES), axis=0, keepdims=True)
    e2 = jnp.exp(v2 - v1)
    w1 = g_prob / (1.0 + e2)
    w2 = g_prob * e2 / (1.0 + e2)
    eidx = jnp.where(sub == 0, g_idx * EXPERTS_PER_GROUP + i1,
                     jnp.where(sub == 1, g_idx * EXPERTS_PER_GROUP + i2, 0))
    eidx_ref[...] = eidx
    wrow = jnp.where(sub == 0, w1, jnp.where(sub == 1, w2, 0.0))
    wfull = jnp.concatenate([wrow, jnp.zeros((LANES - SUBLANES, tm), F32)], axis=0)
    wt_ref[...] = wfull.T


def moe_router(tiles, n_tiles, x, gain, mods, grp_w, grp_b, exp_w, exp_b):
    d = x.shape[1]
    tm = tiles.tm
    n_tok = n_tiles * tm
    wr = jnp.concatenate([grp_w.T, jnp.zeros((SUBLANES - N_GROUPS, d), F32), exp_w.T], axis=0)
    br = jnp.concatenate([grp_b, jnp.full((SUBLANES - N_GROUPS,), NEG_BIG, F32), exp_b]).reshape(ROUTER_ROWS, 1)
    return pl.pallas_call(
        functools.partial(_router_kernel, tiles),
        out_shape=[jax.ShapeDtypeStruct((n_tok * SUBLANES, LANES), F32),
                   jax.ShapeDtypeStruct((SUBLANES, n_tok), jnp.int32),
                   jax.ShapeDtypeStruct((n_tok, LANES), F32)],
        grid=(n_tiles,),
        in_specs=[
            pl.BlockSpec((tm, d), lambda i: (i, 0)),
            pl.BlockSpec((1, d), lambda i: (0, 0)),
            pl.BlockSpec(mods.shape, lambda i: (0, 0)),
            pl.BlockSpec((ROUTER_ROWS, d), lambda i: (0, 0)),
            pl.BlockSpec((ROUTER_ROWS, 1), lambda i: (0, 0)),
        ],
        out_specs=[pl.BlockSpec((tm * SUBLANES, LANES), lambda i: (i, 0)),
                   pl.BlockSpec((SUBLANES, tm), lambda i: (0, i)),
                   pl.BlockSpec((tm, LANES), lambda i: (i, 0))],
        compiler_params=_cparams(("parallel",)),
        name="moe_router",
    )(x, gain.reshape(1, d), mods, wr, br)


def _route_plan(eidx, n_tok, te):
    n_asg = 2 * n_tok
    n_blocks = (n_asg + N_EXPERTS * (te - 1)) // te
    n_rows = n_blocks * te
    e = eidx[:2].reshape(n_asg)
    onehot = (e[:, None] == jnp.arange(N_EXPERTS, dtype=jnp.int32)[None, :]).astype(jnp.int32)
    csum = jnp.cumsum(onehot, axis=0)
    rank = jnp.sum(csum * onehot, axis=1) - 1
    counts = csum[-1]
    padded = (counts + te - 1) // te * te
    pend = jnp.cumsum(padded)
    pstart = pend - padded
    dest = pstart[e] + rank
    asg = jnp.arange(n_asg, dtype=jnp.int32)
    row_asg = jnp.full((n_rows,), -1, jnp.int32).at[dest].set(asg)
    is_pad = row_asg < 0
    pad_rank = jnp.cumsum(is_pad.astype(jnp.int32)) - 1
    row_src = jnp.where(is_pad, 0, row_asg % n_tok)
    row_dst = jnp.where(is_pad, n_asg + pad_rank, row_asg)
    block_exp = jnp.minimum(jnp.searchsorted(pend, jnp.arange(n_blocks, dtype=jnp.int32) * te, side='right'),
                            N_EXPERTS - 1).astype(jnp.int32)
    n_used = (pend[-1] // te).astype(jnp.int32).reshape(1)
    return block_exp, n_used, row_src.reshape(n_blocks, 1, te), row_dst.reshape(n_blocks, 1, te), n_rows


def _expert_kernel(te, n_blocks, bexp_ref, nused_ref, src_cur, src_next, dst_cur, h3_hbm, wg_ref, wu_ref, wd_ref,
                   y3_hbm, xbuf, ybuf, gsem, ssem, wgb, wub, wdb):
    i = pl.program_id(0)
    slot = i % 2
    tile = SUBLANES

    def gather_copy(tok, r, s):
        return pltpu.make_async_copy(h3_hbm.at[pl.ds(tok * tile, tile), :],
                                     xbuf.at[s, pl.ds(r * tile, tile), :], gsem.at[s])

    def scatter_copy(row, r, s):
        return pltpu.make_async_copy(ybuf.at[s, pl.ds(r * tile, tile), :],
                                     y3_hbm.at[pl.ds(row * tile, tile), :], ssem.at[s])

    def start_gather(src_ref, s):
        def body(r, c):
            gather_copy(src_ref[0, r], r, s).start()
            return c
        lax.fori_loop(0, te, body, 0, unroll=8)

    def wait_gather(s):
        def body(r, c):
            gather_copy(0, r, s).wait()
            return c
        lax.fori_loop(0, te, body, 0, unroll=8)

    def start_scatter(s):
        def body(r, c):
            scatter_copy(dst_cur[0, r], r, s).start()
            return c
        lax.fori_loop(0, te, body, 0, unroll=8)

    def wait_scatter(s):
        def body(r, c):
            scatter_copy(0, r, s).wait()
            return c
        lax.fori_loop(0, te, body, 0, unroll=8)

    @pl.when(i == 0)
    def _():
        start_gather(src_cur, 0)

    @pl.when(i + 1 < n_blocks)
    def _():
        start_gather(src_next, 1 - slot)

    wait_gather(slot)

    @pl.when(i >= 2)
    def _():
        wait_scatter(slot)

    @pl.when(i < nused_ref[0])
    def _():
        prev = bexp_ref[jnp.maximum(i - 1, 0)]

        @pl.when(jnp.logical_or(i == 0, bexp_ref[i] != prev))
        def _():
            wgb[...] = wg_ref[...].astype(BF16)
            wub[...] = wu_ref[...].astype(BF16)
            wdb[...] = wd_ref[...].astype(BF16)

        xb = _from_token_tiles(xbuf.at[slot], te).astype(BF16)
        hg = _dot(xb, wgb[...])
        hu = _dot(xb, wub[...])
        act = (_silu(hg) * hu).astype(BF16)
        _to_token_tiles(ybuf.at[slot], _dot(act, wdb[...]))

    start_scatter(slot)

    @pl.when(i == n_blocks - 1)
    def _():
        wait_scatter(slot)
        if n_blocks > 1:
            wait_scatter(1 - slot)


def moe_experts(layer, h3, plan, n_tok, w_gate, w_up, w_down, te):
    block_exp, n_used, row_src, row_dst, n_rows = plan
    n_blocks = block_exp.shape[0]
    _, _, d, ff = w_gate.shape
    smem_blk = lambda f: pl.BlockSpec((None, 1, te), f, memory_space=pltpu.SMEM)
    grid_spec = pltpu.PrefetchScalarGridSpec(
        num_scalar_prefetch=2,
        grid=(n_blocks,),
        in_specs=[
            smem_blk(lambda i, be, nu: (i, 0, 0)),
            smem_blk(lambda i, be, nu: (jnp.minimum(i + 1, n_blocks - 1), 0, 0)),
            smem_blk(lambda i, be, nu: (i, 0, 0)),
            pl.BlockSpec(memory_space=pl.ANY),
            pl.BlockSpec((None, None, d, ff), lambda i, be, nu: (layer, be[i], 0, 0)),
            pl.BlockSpec((None, None, d, ff), lambda i, be, nu: (layer, be[i], 0, 0)),
            pl.BlockSpec((None, None, ff, d), lambda i, be, nu: (layer, be[i], 0, 0)),
        ],
        out_specs=pl.BlockSpec(memory_space=pl.ANY),
        scratch_shapes=[
            pltpu.VMEM((2, te * SUBLANES, LANES), F32),
            pltpu.VMEM((2, te * SUBLANES, LANES), F32),
            pltpu.SemaphoreType.DMA((2,)),
            pltpu.SemaphoreType.DMA((2,)),
            pltpu.VMEM((d, ff), BF16),
            pltpu.VMEM((d, ff), BF16),
            pltpu.VMEM((ff, d), BF16),
        ],
    )
    return pl.pallas_call(
        functools.partial(_expert_kernel, te, n_blocks),
        out_shape=jax.ShapeDtypeStruct((n_rows * SUBLANES, LANES), F32),
        grid_spec=grid_spec,
        compiler_params=_cparams(("arbitrary",)),
        name="moe_experts",
    )(block_exp, n_used, row_src, row_src, row_dst, h3, w_gate, w_up, w_down)


def _combine_kernel(tiles, final, x_ref, y0_ref, y1_ref, wt_ref, mod_ref, *rest):
    if final:
        fg_ref, o_ref = rest
    else:
        (o_ref,) = rest
    i = pl.program_id(0)
    tm, d = x_ref.shape
    gate = mod_ref[pl.ds(tiles.mod_row(i), 1), 5 * d:6 * d]
    wt = wt_ref[...]
    y = wt[:, 0:1] * _from_token_tiles(y0_ref, tm) + wt[:, 1:2] * _from_token_tiles(y1_ref, tm)
    out = x_ref[...] + gate * y
    if final:
        out = _rms(out) * fg_ref[...]
    o_ref[...] = out


def moe_combine(tiles, n_tiles, x, y3, wt, mods, final_gain):
    tm = tiles.tm
    d = x.shape[1]
    final = final_gain is not None
    in_specs = [
        pl.BlockSpec((tm, d), lambda i: (i, 0)),
        pl.BlockSpec((tm * SUBLANES, LANES), lambda i: (i, 0)),
        pl.BlockSpec((tm * SUBLANES, LANES), lambda i: (n_tiles + i, 0)),
        pl.BlockSpec((tm, LANES), lambda i: (i, 0)),
        pl.BlockSpec(mods.shape, lambda i: (0, 0)),
    ]
    args = [x, y3, y3, wt, mods]
    if final:
        in_specs.append(pl.BlockSpec((1, d), lambda i: (0, 0)))
        args.append(final_gain.reshape(1, d))
    return pl.pallas_call(
        functools.partial(_combine_kernel, tiles, final),
        out_shape=jax.ShapeDtypeStruct((n_tiles * tm, d), F32),
        grid=(n_tiles,),
        in_specs=in_specs,
        out_specs=pl.BlockSpec((tm, d), lambda i: (i, 0)),
        compiler_params=_cparams(("parallel",)),
        name="moe_combine",
    )(*args)


def moe_layer(tiles, layer, n_tiles, xs, gain, mods, grp_w, grp_b, exp_w, exp_b, w_gate, w_up, w_down, final_gain):
    n_tok = n_tiles * tiles.tm
    h3, eidx, wt = moe_router(tiles, n_tiles, xs, gain, mods, grp_w, grp_b, exp_w, exp_b)
    plan = _route_plan(eidx, n_tok, EXPERT_TILE)
    y3 = moe_experts(layer, h3, plan, n_tok, w_gate, w_up, w_down, EXPERT_TILE)
    return moe_combine(tiles, n_tiles, xs, y3, wt, mods, final_gain)


def kernel(x, c, ctx, c_ctx, mod_w, mod_b, norm_mix_g, norm_ffn_g, w_in_ab, conv_w, conv_b, conv_norm_g,
           w_out_ab, w_in_cd, q_norm_g, k_norm_g, cq_norm_g, ckv_norm_g, w_uq, w_ukv, w_out_cd,
           router_grp_w, router_grp_b, router_exp_w, router_exp_b, exp_w_gate, exp_w_up, exp_w_down,
           final_norm_g):
    n_batch, seq, d = x.shape
    ctx_len = ctx.shape[1]
    depth = mod_w.shape[0]
    tiles = _Tiles(n_batch, seq, ctx_len)
    xs = jnp.concatenate([x.reshape(n_batch * seq, d), ctx.reshape(n_batch * ctx_len, d)], axis=0)
    cvec = jnp.zeros((SUBLANES, d), F32).at[:n_batch].set(c).at[n_batch].set(c_ctx)
    mods = compute_mods(cvec, mod_w, mod_b)
    for layer in range(depth):
        last = layer == depth - 1
        j = layer // 2
        lmods = mods[layer]
        if layer % 2 == 0:
            pa, pug = norm_proj(xs, norm_mix_g[layer], lmods, w_in_ab[j].astype(BF16),
                                (A_WIDTH, 2 * B_WIDTH), 0, seq // 512, n_batch)
            ya_lat, ya_ctx = fourier_mix(pa, n_batch, seq, ctx_len, not last)
            xs = even_out(tiles, xs, ya_lat, ya_ctx, pug, conv_w[j], conv_b[j], conv_norm_g[j],
                          w_out_ab[j].astype(BF16), lmods)
        else:
            xs = attn_layer(tiles, xs, norm_mix_g[layer], lmods, w_in_cd[j].astype(BF16), q_norm_g[j], k_norm_g[j],
                            cq_norm_g[j], ckv_norm_g[j], _permute_w_uq(w_uq[j]).astype(BF16),
                            w_ukv[j].astype(BF16), w_out_cd[j].astype(BF16), not last)
        n_tiles = tiles.n_lat if last else tiles.n_tiles
        xs = moe_layer(tiles, layer, n_tiles, xs, norm_ffn_g[layer], lmods, router_grp_w[layer], router_grp_b[layer],
                       router_exp_w[layer], router_exp_b[layer], exp_w_gate, exp_w_up, exp_w_down,
                       final_norm_g if last else None)
    return xs.reshape(n_batch, seq, d)
```

```python
import functools
import math

import numpy as np
import jax
import jax.numpy as jnp
from jax import lax
from jax.experimental import pallas as pl
from jax.experimental.pallas import tpu as pltpu

F32 = jnp.float32
BF16 = jnp.bfloat16

D_MODEL = 1024
GRID_W = 64
A_GROUPS = 4
A_GROUP_DIM = 128
A_WIDTH = 512
B_WIDTH = 512
CONV_WIDTH = 31
CONV_PAD = 15
C_HEADS = 4
C_KV_HEADS = 2
C_HEAD_DIM = 128
C_WIDTH = 512
C_KV_WIDTH = 256
D_HEADS = 4
D_NOPE = 128
D_ROPE = 64
D_QK = D_NOPE + D_ROPE
D_V = 128
Q_LORA = 384
KV_LORA = 256
D_WIDTH = 512
N_GROUPS = 4
EXPERTS_PER_GROUP = 8
N_EXPERTS = 32
EXPERT_FF = 512
N_MOD = 6
RMS_EPS = 1e-6
ROPE_THETA = 10000.0
LOG2E = math.log2(math.e)

LANES = 128
SUBLANES = 8
ROW_TILE = 256
HALO = 16
FOURIER_N2 = 128
EXPERT_TILE = 256
VMEM_LIMIT = 56 * 1024 * 1024


def _cparams(sem):
    return pltpu.CompilerParams(dimension_semantics=sem, vmem_limit_bytes=VMEM_LIMIT)


def _split3(x):
    hi = x.astype(BF16)
    r1 = x - hi.astype(F32)
    mid = r1.astype(BF16)
    lo = (r1 - mid.astype(F32)).astype(BF16)
    return hi, mid, lo


def _dot(a, b):
    return jnp.dot(a, b, preferred_element_type=F32)


def _dot_nt(a, b):
    return lax.dot_general(a, b, (((1,), (1,)), ((), ())), preferred_element_type=F32)


def _dot_precise(a, b):
    a0, a1, a2 = _split3(a)
    b0, b1, b2 = _split3(b)
    out = _dot(a0, b0)
    out += _dot(a0, b1) + _dot(a1, b0)
    out += _dot(a0, b2) + _dot(a1, b1) + _dot(a2, b0)
    return out


def _dot_nt_precise(a, b):
    a0, a1, a2 = _split3(a)
    b0, b1, b2 = _split3(b)
    out = _dot_nt(a0, b0)
    out += _dot_nt(a0, b1) + _dot_nt(a1, b0)
    out += _dot_nt(a0, b2) + _dot_nt(a1, b1) + _dot_nt(a2, b0)
    return out


def _silu(x):
    return x * (1.0 / (1.0 + jnp.exp(-x)))


def _rms(x):
    return x * lax.rsqrt(jnp.mean(x * x, axis=-1, keepdims=True) + RMS_EPS)


def _mods_kernel(c_ref, w_ref, b_ref, o_ref):
    s = _silu(c_ref[...])
    o_ref[0] = _dot_precise(s, w_ref[0]) + b_ref[0]


def compute_mods(cvec, mod_w, mod_b):
    depth, d, n = mod_w.shape
    tn = 1024
    return pl.pallas_call(
        _mods_kernel,
        out_shape=jax.ShapeDtypeStruct((depth, SUBLANES, n), F32),
        grid=(depth, n // tn),
        in_specs=[
            pl.BlockSpec((SUBLANES, d), lambda l, j: (0, 0)),
            pl.BlockSpec((1, d, tn), lambda l, j: (l, 0, j)),
            pl.BlockSpec((1, 1, tn), lambda l, j: (l, 0, j)),
        ],
        out_specs=pl.BlockSpec((1, SUBLANES, tn), lambda l, j: (l, 0, j)),
        compiler_params=_cparams(("parallel", "parallel")),
        name="mods",
    )(cvec, mod_w, mod_b.reshape(depth, 1, n))


def _mod_row(i, tiles_per_batch, n_batch):
    return jnp.minimum(i // tiles_per_batch, n_batch)


def _norm_proj_kernel(splits, tiles_per_batch, n_batch, chunk, x_ref, g_ref, mod_ref, w_ref, *o_refs):
    i = pl.program_id(0)
    r = _mod_row(i, tiles_per_batch, n_batch)
    d = x_ref.shape[1]
    sh = mod_ref[pl.ds(r, 1), chunk * d:(chunk + 1) * d]
    sc = mod_ref[pl.ds(r, 1), (chunk + 1) * d:(chunk + 2) * d]
    h = _rms(x_ref[...]) * g_ref[...] * (1.0 + sc) + sh
    p = _dot(h.astype(BF16), w_ref[...])
    off = 0
    for o_ref, n in zip(o_refs, splits):
        o_ref[...] = p[:, off:off + n].astype(o_ref.dtype)
        off += n


def norm_proj(x, gain, mods, w_bf16, splits, chunk, tiles_per_batch, n_batch, tm=512):
    t, d = x.shape
    n = w_bf16.shape[1]
    assert sum(splits) == n and t % tm == 0
    kern = functools.partial(_norm_proj_kernel, tuple(splits), tiles_per_batch, n_batch, chunk)
    return pl.pallas_call(
        kern,
        out_shape=[jax.ShapeDtypeStruct((t, s), F32) for s in splits],
        grid=(t // tm,),
        in_specs=[
            pl.BlockSpec((tm, d), lambda i: (i, 0)),
            pl.BlockSpec((1, d), lambda i: (0, 0)),
            pl.BlockSpec(mods.shape, lambda i: (0, 0)),
            pl.BlockSpec((d, n), lambda i: (0, 0)),
        ],
        out_specs=[pl.BlockSpec((tm, s), lambda i: (i, 0)) for s in splits],
        compiler_params=_cparams(("parallel",)),
        name="norm_proj",
    )(x, gain.reshape(1, d), mods, w_bf16)


def _dft_tables(n1, n2, group_dim):
    ell = n1 * n2
    k1 = np.arange(n1, dtype=np.float64)
    ang1 = 2.0 * np.pi * np.outer(k1, k1) / n1
    f1 = np.concatenate([np.cos(ang1), -np.sin(ang1)], axis=0) / np.sqrt(n1)
    l2 = np.arange(n2, dtype=np.float64)
    k2 = np.arange(n2, dtype=np.float64)
    theta = 2.0 * np.pi * (k2[None, :, None] * l2[None, None, :] / n2
                           + k1[:, None, None] * l2[None, None, :] / ell)
    gr, gi = np.cos(theta), -np.sin(theta)
    g = np.concatenate([np.concatenate([gr, -gi], axis=2),
                        np.concatenate([gi, gr], axis=2)], axis=1) / np.sqrt(n2)
    c = np.arange(group_dim, dtype=np.float64)
    angc = 2.0 * np.pi * np.outer(c, c) / group_dim
    cs = np.concatenate([np.cos(angc), np.sin(angc)], axis=0) / np.sqrt(group_dim)
    return (jnp.asarray(f1, F32).astype(BF16), jnp.asarray(g, F32).astype(BF16),
            jnp.asarray(cs, F32).astype(BF16))


DFT_L2_BLOCK = 16


def _dft_stage1_kernel(f_ref, x_ref, o_ref):
    f = f_ref[...]
    for s in range(x_ref.shape[1]):
        o_ref[0, :, s, :] = _dot(f, x_ref[:, s, :].astype(BF16)).astype(o_ref.dtype)


def dft_stage1(f1, x3, n_batch, n1):
    _, n2, width = x3.shape
    sb = DFT_L2_BLOCK
    return pl.pallas_call(
        _dft_stage1_kernel,
        out_shape=jax.ShapeDtypeStruct((n_batch, 2 * n1, n2, width), BF16),
        grid=(n_batch, n2 // sb),
        in_specs=[
            pl.BlockSpec((2 * n1, n1), lambda b, j: (0, 0)),
            pl.BlockSpec((n1, sb, width), lambda b, j: (b, j, 0)),
        ],
        out_specs=pl.BlockSpec((1, 2 * n1, sb, width), lambda b, j: (b, 0, j, 0)),
        compiler_params=_cparams(("parallel", "parallel")),
        name="dft_stage1",
    )(f1, x3)


def _dft_stage2_kernel(kb, n2, real_input, *refs):
    if real_input:
        g_ref, cs_ref, yr_ref, o_ref = refs
    else:
        g_ref, cs_ref, yr_ref, yi_ref, o_ref = refs
    width = yr_ref.shape[-1]
    gd = cs_ref.shape[1]
    for k in range(kb):
        if real_input:
            z = _dot(g_ref[k, :, :n2], yr_ref[0, k].astype(BF16))
        else:
            y = jnp.concatenate([yr_ref[0, k].astype(BF16), yi_ref[0, k].astype(BF16)], axis=0)
            z = _dot(g_ref[k], y)
        zr, zi = z[:n2], z[n2:]
        for grp in range(width // gd):
            sl = slice(grp * gd, (grp + 1) * gd)
            lhs = jnp.concatenate([zr[:, sl], zi[:, sl]], axis=1).astype(BF16)
            o_ref[0, :, k, sl] = _dot(lhs, cs_ref[...]).astype(o_ref.dtype)


def dft_stage2(g, cs, y4, n1, n2, real_input, kb):
    n_batch, _, _, width = y4.shape
    kern = functools.partial(_dft_stage2_kernel, kb, n2, real_input)
    in_specs = [
        pl.BlockSpec((kb, 2 * n2, 2 * n2), lambda b, j: (j, 0, 0)),
        pl.BlockSpec(cs.shape, lambda b, j: (0, 0)),
        pl.BlockSpec((1, kb, n2, width), lambda b, j: (b, j, 0, 0)),
    ]
    args = [g, cs, y4]
    if not real_input:
        nj = n1 // kb
        in_specs.append(pl.BlockSpec((1, kb, n2, width), lambda b, j: (b, nj + j, 0, 0)))
        args.append(y4)
    return pl.pallas_call(
        kern,
        out_shape=jax.ShapeDtypeStruct((n_batch, n2, n1, width), BF16),
        grid=(n_batch, n1 // kb),
        in_specs=in_specs,
        out_specs=pl.BlockSpec((1, n2, kb, width), lambda b, j: (b, 0, j, 0)),
        compiler_params=_cparams(("parallel", "parallel")),
        name="dft_stage2",
    )(*args)


def fourier_mix(pa, n_batch, seq, ctx_len, with_ctx):
    t, width = pa.shape
    n2 = FOURIER_N2
    n1 = seq // n2
    f1, g, cs = _dft_tables(n1, n2, A_GROUP_DIM)
    y4 = dft_stage1(f1, pa.reshape(t // n2, n2, width), n_batch, n1)
    kb = min(16, n1)
    ya_lat = dft_stage2(g, cs, y4, n1, n2, False, kb).reshape(n_batch * seq, width)
    ya_ctx = None
    if with_ctx:
        _, gc, _ = _dft_tables(1, ctx_len, A_GROUP_DIM)
        xc = pa[n_batch * seq:].reshape(n_batch, 1, ctx_len, width)
        ya_ctx = dft_stage2(gc, cs, xc, 1, ctx_len, True, 1).reshape(n_batch * ctx_len, width)
    return ya_lat, ya_ctx


class _Tiles:
    def __init__(self, n_batch, seq, ctx_len, tm=ROW_TILE):
        assert seq % tm == 0 and ctx_len % tm == 0
        self.tm = tm
        self.n_batch = n_batch
        self.seq = seq
        self.ctx_len = ctx_len
        self.tps = seq // tm
        self.ctps = ctx_len // tm
        self.n_lat = n_batch * self.tps
        self.n_ctx = n_batch * self.ctps
        self.n_tiles = self.n_lat + self.n_ctx
        self.n_tok = self.n_tiles * tm

    def mod_row(self, i):
        return jnp.minimum(i // self.tps, self.n_batch)

    def lat_index(self, i):
        return jnp.minimum(i, self.n_lat - 1)

    def ctx_index(self, i):
        return jnp.maximum(i - self.n_lat, 0)

    def seq_pos(self, i):
        ic = i - self.n_lat
        is_ctx = i >= self.n_lat
        b = jnp.where(is_ctx, ic // self.ctps, i // self.tps)
        j = jnp.where(is_ctx, self.tps + ic % self.ctps, i % self.tps)
        return b, j

    def seq_first(self, i):
        ic = i - self.n_lat
        return jnp.where(i >= self.n_lat, ic % self.ctps == 0, i % self.tps == 0)

    def seq_last(self, i):
        ic = i - self.n_lat
        return jnp.where(i >= self.n_lat, ic % self.ctps == self.ctps - 1, i % self.tps == self.tps - 1)


CONV_CHUNK = 32


def _even_out_kernel(tiles, x_ref, yal_ref, yac_ref, ug_ref, prev_ref, next_ref, cw_ref, cb_ref, cg_ref,
                     w_ref, mod_ref, o_ref, ubuf, ybuf):
    i = pl.program_id(0)
    tm = tiles.tm
    bw = B_WIDTH
    d = x_ref.shape[1]

    def glu(ref):
        v = ref[...]
        return v[:, :bw] * (1.0 / (1.0 + jnp.exp(-v[:, bw:])))

    first = tiles.seq_first(i)
    last = tiles.seq_last(i)
    ubuf[0:HALO, :] = jnp.where(first, 0.0, glu(prev_ref))
    ubuf[HALO:HALO + tm, :] = glu(ug_ref)
    ubuf[HALO + tm:HALO + tm + HALO, :] = jnp.where(last, 0.0, glu(next_ref))

    ya = jnp.where(i >= tiles.n_lat, yac_ref[...], yal_ref[...])
    r = tiles.mod_row(i)
    gate = mod_ref[pl.ds(r, 1), 2 * d:3 * d]
    cb = cb_ref[...]
    cg = cg_ref[...]
    base = HALO - CONV_PAD
    for r0 in range(0, tm, CONV_CHUNK):
        acc = jnp.zeros((CONV_CHUNK, bw), F32)
        for k in range(CONV_WIDTH):
            acc = acc + cw_ref[k:k + 1, :] * ubuf[r0 + base + k:r0 + base + k + CONV_CHUNK, :]
        ybuf[r0:r0 + CONV_CHUNK, :] = _silu(_rms(acc + cb) * cg).astype(BF16)
    y = _dot(ya, w_ref[:A_WIDTH, :]) + _dot(ybuf[...], w_ref[A_WIDTH:, :])
    o_ref[...] = x_ref[...] + gate * y


def even_out(tiles, x, ya_lat, ya_ctx, pug, conv_w, conv_b, conv_g, w_out_bf16, mods):
    t, d = x.shape
    tm = tiles.tm
    n_tiles = t // tm
    hb = tm // HALO
    n_hblk = t // HALO
    if ya_ctx is None:
        ya_ctx = ya_lat
    cw = jnp.concatenate([conv_w, jnp.zeros((1, B_WIDTH), F32)], axis=0)
    kern = functools.partial(_even_out_kernel, tiles)
    return pl.pallas_call(
        kern,
        out_shape=jax.ShapeDtypeStruct((t, d), F32),
        grid=(n_tiles,),
        in_specs=[
            pl.BlockSpec((tm, d), lambda i: (i, 0)),
            pl.BlockSpec((tm, A_WIDTH), lambda i: (tiles.lat_index(i), 0)),
            pl.BlockSpec((tm, A_WIDTH), lambda i: (tiles.ctx_index(i), 0)),
            pl.BlockSpec((tm, 2 * B_WIDTH), lambda i: (i, 0)),
            pl.BlockSpec((HALO, 2 * B_WIDTH), lambda i: (jnp.maximum(i * hb - 1, 0), 0)),
            pl.BlockSpec((HALO, 2 * B_WIDTH), lambda i: (jnp.minimum((i + 1) * hb, n_hblk - 1), 0)),
            pl.BlockSpec((CONV_WIDTH + 1, B_WIDTH), lambda i: (0, 0)),
            pl.BlockSpec((1, B_WIDTH), lambda i: (0, 0)),
            pl.BlockSpec((1, B_WIDTH), lambda i: (0, 0)),
            pl.BlockSpec((d, d), lambda i: (0, 0)),
            pl.BlockSpec(mods.shape, lambda i: (0, 0)),
        ],
        out_specs=pl.BlockSpec((tm, d), lambda i: (i, 0)),
        scratch_shapes=[pltpu.VMEM((tm + 2 * HALO, B_WIDTH), F32), pltpu.VMEM((tm, B_WIDTH), BF16)],
        compiler_params=_cparams(("parallel",)),
        name="even_out",
    )(x, ya_lat, ya_ctx, pug, pug, pug, cw, conv_b.reshape(1, B_WIDTH), conv_g.reshape(1, B_WIDTH),
      w_out_bf16, mods)


def _rope_tables(seq, tm, dim):
    n_freq = dim // 4
    pos = jnp.arange(seq, dtype=jnp.int32)
    row = (pos // GRID_W).astype(F32)
    col = (pos % GRID_W).astype(F32)
    inv = ROPE_THETA ** (-jnp.arange(n_freq, dtype=F32) / n_freq)
    ang = jnp.concatenate([row[:, None] * inv[None, :], col[:, None] * inv[None, :]], axis=-1)
    cos = jnp.repeat(jnp.cos(ang), 2, axis=-1)
    sin = jnp.repeat(jnp.sin(ang), 2, axis=-1)
    sign = jnp.tile(jnp.asarray([-1.0, 1.0], F32), dim // 2)
    cos = jnp.concatenate([cos, jnp.ones((tm, dim), F32)], axis=0)
    sin = jnp.concatenate([sin * sign, jnp.zeros((tm, dim), F32)], axis=0)
    return cos, sin


def _pair_swap(x):
    n = x.shape[1]
    lane = lax.broadcasted_iota(jnp.int32, x.shape, 1)
    nxt = pltpu.roll(x, n - 1, 1)
    prv = pltpu.roll(x, 1, 1)
    return jnp.where((lane & 1) == 0, nxt, prv)


def _rope(x, cos, sin):
    parts = []
    for a in range(x.shape[1] // LANES):
        xa = x[:, a * LANES:(a + 1) * LANES]
        parts.append(xa * cos + _pair_swap(xa) * sin)
    return parts[0] if len(parts) == 1 else jnp.concatenate(parts, axis=1)


def _attn_prep_kernel(p_ref, cosc_ref, sinc_ref, cosd_ref, sind_ref, qg_ref, kg_ref, cqg_ref, ckvg_ref,
                      wuq_ref, wukv_ref, qc_ref, kc_ref, vct_ref, qd_ref, kd_ref, vdt_ref):
    hd = C_HEAD_DIM
    p = p_ref[...]
    o_k = C_WIDTH
    o_v = o_k + C_KV_WIDTH
    o_cq = o_v + C_KV_WIDTH
    o_ckv = o_cq + Q_LORA
    o_kr = o_ckv + KV_LORA
    cosc, sinc = cosc_ref[...], sinc_ref[...]
    for h in range(C_HEADS):
        qh = _rms(p[:, h * hd:(h + 1) * hd]) * qg_ref[...]
        qc_ref[h] = (_rope(qh, cosc, sinc) * (LOG2E * hd ** -0.5)).astype(qc_ref.dtype)
    for h in range(C_KV_HEADS):
        kh = _rms(p[:, o_k + h * hd:o_k + (h + 1) * hd]) * kg_ref[...]
        kc_ref[h] = _rope(kh, cosc, sinc).astype(kc_ref.dtype)
        vct_ref[h] = p[:, o_v + h * hd:o_v + (h + 1) * hd].T.astype(vct_ref.dtype)
    cosd = jnp.concatenate([cosd_ref[...], cosd_ref[...]], axis=1)
    sind = jnp.concatenate([sind_ref[...], sind_ref[...]], axis=1)
    cq = (_rms(p[:, o_cq:o_cq + Q_LORA]) * cqg_ref[...]).astype(BF16)
    qd = _dot(cq, wuq_ref[...]) * (LOG2E * D_QK ** -0.5)
    qr = _rope(qd[:, D_HEADS * D_NOPE:], cosd, sind)
    ckv = (_rms(p[:, o_ckv:o_ckv + KV_LORA]) * ckvg_ref[...]).astype(BF16)
    kvd = _dot(ckv, wukv_ref[...])
    kr2 = jnp.concatenate([p[:, o_kr:o_kr + D_ROPE], p[:, o_kr:o_kr + D_ROPE]], axis=1)
    kr = _rope(kr2, cosd, sind)[:, :D_ROPE]
    for h in range(D_HEADS):
        qd_ref[h, :, :D_NOPE] = qd[:, h * D_NOPE:(h + 1) * D_NOPE].astype(qd_ref.dtype)
        qd_ref[h, :, D_NOPE:] = qr[:, h * D_ROPE:(h + 1) * D_ROPE].astype(qd_ref.dtype)
        kd_ref[h, :, :D_NOPE] = kvd[:, h * 2 * D_NOPE:h * 2 * D_NOPE + D_NOPE].astype(kd_ref.dtype)
        kd_ref[h, :, D_NOPE:] = kr.astype(kd_ref.dtype)
        vdt_ref[h] = kvd[:, h * 2 * D_NOPE + D_NOPE:(h + 1) * 2 * D_NOPE].T.astype(vdt_ref.dtype)


def attn_prep(tiles, p, q_g, k_g, cq_g, ckv_g, w_uq_bf16, w_ukv_bf16):
    tm = tiles.tm
    nb = tiles.n_batch
    stot = tiles.seq + tiles.ctx_len
    cosc, sinc = _rope_tables(tiles.seq, tm, C_HEAD_DIM)
    cosd, sind = _rope_tables(tiles.seq, tm, D_ROPE)

    def rope_idx(i):
        return (jnp.where(i >= tiles.n_lat, tiles.tps, i % tiles.tps), 0)

    def rows_idx(i):
        b, j = tiles.seq_pos(i)
        return (b, 0, j, 0)

    def cols_idx(i):
        b, j = tiles.seq_pos(i)
        return (b, 0, 0, j)

    const = lambda i: (0, 0)
    out_shape = [
        jax.ShapeDtypeStruct((nb, C_HEADS, stot, C_HEAD_DIM), BF16),
        jax.ShapeDtypeStruct((nb, C_KV_HEADS, stot, C_HEAD_DIM), BF16),
        jax.ShapeDtypeStruct((nb, C_KV_HEADS, C_HEAD_DIM, stot), BF16),
        jax.ShapeDtypeStruct((nb, D_HEADS, stot, D_QK), BF16),
        jax.ShapeDtypeStruct((nb, D_HEADS, stot, D_QK), BF16),
        jax.ShapeDtypeStruct((nb, D_HEADS, D_V, stot), BF16),
    ]
    out_specs = [
        pl.BlockSpec((None, C_HEADS, tm, C_HEAD_DIM), rows_idx),
        pl.BlockSpec((None, C_KV_HEADS, tm, C_HEAD_DIM), rows_idx),
        pl.BlockSpec((None, C_KV_HEADS, C_HEAD_DIM, tm), cols_idx),
        pl.BlockSpec((None, D_HEADS, tm, D_QK), rows_idx),
        pl.BlockSpec((None, D_HEADS, tm, D_QK), rows_idx),
        pl.BlockSpec((None, D_HEADS, D_V, tm), cols_idx),
    ]
    return pl.pallas_call(
        _attn_prep_kernel,
        out_shape=out_shape,
        grid=(tiles.n_tiles,),
        in_specs=[
            pl.BlockSpec((tm, p.shape[1]), lambda i: (i, 0)),
            pl.BlockSpec((tm, C_HEAD_DIM), rope_idx),
            pl.BlockSpec((tm, C_HEAD_DIM), rope_idx),
            pl.BlockSpec((tm, D_ROPE), rope_idx),
            pl.BlockSpec((tm, D_ROPE), rope_idx),
            pl.BlockSpec((1, C_HEAD_DIM), const),
            pl.BlockSpec((1, C_HEAD_DIM), const),
            pl.BlockSpec((1, Q_LORA), const),
            pl.BlockSpec((1, KV_LORA), const),
            pl.BlockSpec(w_uq_bf16.shape, const),
            pl.BlockSpec(w_ukv_bf16.shape, const),
        ],
        out_specs=out_specs,
        compiler_params=_cparams(("parallel",)),
        name="attn_prep",
    )(p, cosc, sinc, cosd, sind, q_g.reshape(1, -1), k_g.reshape(1, -1), cq_g.reshape(1, -1),
      ckv_g.reshape(1, -1), w_uq_bf16, w_ukv_bf16)


def _permute_w_uq(w_uq):
    w = w_uq.reshape(Q_LORA, D_HEADS, D_QK)
    return jnp.concatenate([w[:, :, :D_NOPE].reshape(Q_LORA, D_HEADS * D_NOPE),
                            w[:, :, D_NOPE:].reshape(Q_LORA, D_HEADS * D_ROPE)], axis=1)


def _flash_kernel(sub, q_ref, k_ref, vt_ref, o_ref, m_ref, l_ref, acc_ref):
    ki = pl.program_id(3)
    n_k = pl.num_programs(3)
    tk = k_ref.shape[0]

    @pl.when(ki == 0)
    def _():
        m_ref[...] = jnp.full(m_ref.shape, -jnp.inf, F32)
        l_ref[...] = jnp.zeros(l_ref.shape, F32)
        acc_ref[...] = jnp.zeros(acc_ref.shape, F32)

    q = q_ref[...]
    m, l, acc = m_ref[...], l_ref[...], acc_ref[...]
    n_sub = tk // sub
    s_next = _dot_nt(k_ref[0:sub, :], q)
    for j in range(n_sub):
        s = s_next
        if j + 1 < n_sub:
            s_next = _dot_nt(k_ref[(j + 1) * sub:(j + 2) * sub, :], q)
        m_new = jnp.maximum(m, jnp.max(s, axis=0, keepdims=True))
        alpha = jnp.exp2(m - m_new)
        pr = jnp.exp2(s - m_new)
        l = alpha * l + jnp.sum(pr, axis=0, keepdims=True)
        acc = alpha * acc + _dot(vt_ref[:, j * sub:(j + 1) * sub], pr.astype(BF16))
        m = m_new
    m_ref[...], l_ref[...], acc_ref[...] = m, l, acc

    @pl.when(ki == n_k - 1)
    def _():
        o_ref[...] = (acc * (1.0 / l)).T.astype(o_ref.dtype)


def _pick_tile(n, cap, step=ROW_TILE):
    best = step
    for t in range(step, cap + 1, step):
        if n % t == 0:
            best = t
    return best


FLASH_TQ, FLASH_TK, FLASH_SUB = 512, 8448, 768


def flash_attention(q, k, vt, sq, sk, q_off, k_off, tq_cap=FLASH_TQ, tk_cap=FLASH_TK, sub=FLASH_SUB):
    nb, nh, _, dqk = q.shape
    nkh, dv = vt.shape[1], vt.shape[2]
    grp = nh // nkh
    tq = _pick_tile(math.gcd(sq, q_off) if q_off else sq, tq_cap)
    tk = _pick_tile(math.gcd(sk, k_off) if k_off else sk, tk_cap)
    qo, ko = q_off // tq, k_off // tk
    return pl.pallas_call(
        functools.partial(_flash_kernel, min(sub, tk)),
        out_shape=jax.ShapeDtypeStruct((nb, sq, nh * dv), BF16),
        grid=(nb, nh, sq // tq, sk // tk),
        in_specs=[
            pl.BlockSpec((None, None, tq, dqk), lambda b, h, qi, ki: (b, h, qo + qi, 0)),
            pl.BlockSpec((None, None, tk, dqk), lambda b, h, qi, ki: (b, h // grp, ko + ki, 0)),
            pl.BlockSpec((None, None, dv, tk), lambda b, h, qi, ki: (b, h // grp, 0, ko + ki)),
        ],
        out_specs=pl.BlockSpec((None, tq, dv), lambda b, h, qi, ki: (b, qi, h)),
        scratch_shapes=[pltpu.VMEM((1, tq), F32), pltpu.VMEM((1, tq), F32), pltpu.VMEM((dv, tq), F32)],
        compiler_params=_cparams(("parallel", "parallel", "parallel", "arbitrary")),
        name="flash_attention",
    )(q, k, vt)


def _attn_out_kernel(tiles, with_ctx, x_ref, *refs):
    if with_ctx:
        ocl_ref, odl_ref, occ_ref, odc_ref, w_ref, mod_ref, o_ref = refs
    else:
        ocl_ref, odl_ref, w_ref, mod_ref, o_ref = refs
    i = pl.program_id(0)
    d = x_ref.shape[1]
    oc, od = ocl_ref[...], odl_ref[...]
    if with_ctx:
        is_ctx = i >= tiles.n_lat
        oc = jnp.where(is_ctx, occ_ref[...], oc)
        od = jnp.where(is_ctx, odc_ref[...], od)
    gate = mod_ref[pl.ds(tiles.mod_row(i), 1), 2 * d:3 * d]
    y = _dot(oc, w_ref[:C_WIDTH, :]) + _dot(od, w_ref[C_WIDTH:, :])
    o_ref[...] = x_ref[...] + gate * y


def attn_out(tiles, x, oc_lat, od_lat, oc_ctx, od_ctx, w_out_bf16, mods):
    t, d = x.shape
    tm = tiles.tm
    with_ctx = oc_ctx is not None
    n_out_tiles = tiles.n_tiles if with_ctx else tiles.n_lat
    lat_spec = pl.BlockSpec((tm, C_WIDTH), lambda i: (tiles.lat_index(i), 0))
    ctx_spec = pl.BlockSpec((tm, C_WIDTH), lambda i: (tiles.ctx_index(i), 0))
    in_specs = [pl.BlockSpec((tm, d), lambda i: (i, 0)), lat_spec, lat_spec]
    args = [x, oc_lat, od_lat]
    if with_ctx:
        in_specs += [ctx_spec, ctx_spec]
        args += [oc_ctx, od_ctx]
    in_specs += [pl.BlockSpec((d, d), lambda i: (0, 0)), pl.BlockSpec(mods.shape, lambda i: (0, 0))]
    args += [w_out_bf16, mods]
    return pl.pallas_call(
        functools.partial(_attn_out_kernel, tiles, with_ctx),
        out_shape=jax.ShapeDtypeStruct((n_out_tiles * tm, d), F32),
        grid=(n_out_tiles,),
        in_specs=in_specs,
        out_specs=pl.BlockSpec((tm, d), lambda i: (i, 0)),
        compiler_params=_cparams(("parallel",)),
        name="attn_out",
    )(*args)


def attn_layer(tiles, xs, gain, mods, w_in_bf16, q_g, k_g, cq_g, ckv_g, w_uq_bf16, w_ukv_bf16, w_out_bf16, with_ctx_out):
    nb, seq, ctx_len = tiles.n_batch, tiles.seq, tiles.ctx_len
    (p,) = norm_proj(xs, gain, mods, w_in_bf16, (w_in_bf16.shape[1],), 0, seq // 512, nb)
    qc, kc, vct, qd, kd, vdt = attn_prep(tiles, p, q_g, k_g, cq_g, ckv_g, w_uq_bf16, w_ukv_bf16)
    stot = seq + ctx_len
    oc_lat = flash_attention(qc, kc, vct, seq, stot, 0, 0).reshape(nb * seq, C_WIDTH)
    od_lat = flash_attention(qd, kd, vdt, seq, stot, 0, 0).reshape(nb * seq, D_WIDTH)
    oc_ctx = od_ctx = None
    if with_ctx_out:
        oc_ctx = flash_attention(qc, kc, vct, ctx_len, ctx_len, seq, seq).reshape(nb * ctx_len, C_WIDTH)
        od_ctx = flash_attention(qd, kd, vdt, ctx_len, ctx_len, seq, seq).reshape(nb * ctx_len, D_WIDTH)
    return attn_out(tiles, xs, oc_lat, od_lat, oc_ctx, od_ctx, w_out_bf16, mods)


ROUTER_ROWS = 40
NEG_BIG = -1e30


def _to_token_tiles(ref, val):
    tm = val.shape[0]
    for s in range(SUBLANES):
        ref[pl.ds(s, tm, stride=SUBLANES), :] = val[:, s * LANES:(s + 1) * LANES]


def _from_token_tiles(ref, tm):
    return jnp.concatenate([ref[pl.ds(s, tm, stride=SUBLANES), :] for s in range(SUBLANES)], axis=1)


def _router_kernel(tiles, x_ref, g_ref, mod_ref, wr_ref, br_ref, h3_ref, eidx_ref, wt_ref):
    i = pl.program_id(0)
    d = x_ref.shape[1]
    tm = x_ref.shape[0]
    r = tiles.mod_row(i)
    sh = mod_ref[pl.ds(r, 1), 3 * d:4 * d]
    sc = mod_ref[pl.ds(r, 1), 4 * d:5 * d]
    h2 = _rms(x_ref[...]) * g_ref[...] * (1.0 + sc) + sh
    _to_token_tiles(h3_ref, h2)
    logits = _dot_nt_precise(wr_ref[...], h2) + br_ref[...]
    sub = lax.broadcasted_iota(jnp.int32, (SUBLANES, tm), 0)
    gl = logits[0:SUBLANES]
    gmax = jnp.max(gl, axis=0, keepdims=True)
    g_idx = jnp.min(jnp.where(gl == gmax, sub, SUBLANES), axis=0, keepdims=True)
    g_prob = 1.0 / jnp.sum(jnp.exp(gl - gmax), axis=0, keepdims=True)
    el = jnp.zeros((SUBLANES, tm), F32)
    for g in range(N_GROUPS):
        el = jnp.where(g_idx == g, logits[SUBLANES * (g + 1):SUBLANES * (g + 2)], el)
    v1 = jnp.max(el, axis=0, keepdims=True)
    i1 = jnp.min(jnp.where(el == v1, sub, SUBLANES), axis=0, keepdims=True)
    el2 = jnp.where(sub == i1, NEG_BIG, el)
    v2 = jnp.max(el2, axis=0, keepdims=True)
    i2 = jnp.min(jnp.where(el2 == v2, sub, SUBLANES), axis=0, keepdims=True)
    e2 = jnp.exp(v2 - v1)
    w1 = g_prob / (1.0 + e2)
    w2 = g_prob * e2 / (1.0 + e2)
    eidx = jnp.where(sub == 0, g_idx * EXPERTS_PER_GROUP + i1,
                     jnp.where(sub == 1, g_idx * EXPERTS_PER_GROUP + i2, 0))
    eidx_ref[...] = eidx
    wrow = jnp.where(sub == 0, w1, jnp.where(sub == 1, w2, 0.0))
    wfull = jnp.concatenate([wrow, jnp.zeros((LANES - SUBLANES, tm), F32)], axis=0)
    wt_ref[...] = wfull.T


def moe_router(tiles, n_tiles, x, gain, mods, grp_w, grp_b, exp_w, exp_b):
    d = x.shape[1]
    tm = tiles.tm
    n_tok = n_tiles * tm
    wr = jnp.concatenate([grp_w.T, jnp.zeros((SUBLANES - N_GROUPS, d), F32), exp_w.T], axis=0)
    br = jnp.concatenate([grp_b, jnp.full((SUBLANES - N_GROUPS,), NEG_BIG, F32), exp_b]).reshape(ROUTER_ROWS, 1)
    return pl.pallas_call(
        functools.partial(_router_kernel, tiles),
        out_shape=[jax.ShapeDtypeStruct((n_tok * SUBLANES, LANES), F32),
                   jax.ShapeDtypeStruct((SUBLANES, n_tok), jnp.int32),
                   jax.ShapeDtypeStruct((n_tok, LANES), F32)],
        grid=(n_tiles,),
        in_specs=[
            pl.BlockSpec((tm, d), lambda i: (i, 0)),
            pl.BlockSpec((1, d), lambda i: (0, 0)),
            pl.BlockSpec(mods.shape, lambda i: (0, 0)),
            pl.BlockSpec((ROUTER_ROWS, d), lambda i: (0, 0)),
            pl.BlockSpec((ROUTER_ROWS, 1), lambda i: (0, 0)),
        ],
        out_specs=[pl.BlockSpec((tm * SUBLANES, LANES), lambda i: (i, 0)),
                   pl.BlockSpec((SUBLANES, tm), lambda i: (0, i)),
                   pl.BlockSpec((tm, LANES), lambda i: (i, 0))],
        compiler_params=_cparams(("parallel",)),
        name="moe_router",
    )(x, gain.reshape(1, d), mods, wr, br)


def _route_plan(eidx, n_tok, te):
    n_asg = 2 * n_tok
    n_blocks = (n_asg + N_EXPERTS * (te - 1)) // te
    n_rows = n_blocks * te
    e = eidx[:2].reshape(n_asg)
    onehot = (e[:, None] == jnp.arange(N_EXPERTS, dtype=jnp.int32)[None, :]).astype(jnp.int32)
    csum = jnp.cumsum(onehot, axis=0)
    rank = jnp.sum(csum * onehot, axis=1) - 1
    counts = csum[-1]
    padded = (counts + te - 1) // te * te
    pend = jnp.cumsum(padded)
    pstart = pend - padded
    dest = pstart[e] + rank
    asg = jnp.arange(n_asg, dtype=jnp.int32)
    row_asg = jnp.full((n_rows,), -1, jnp.int32).at[dest].set(asg)
    is_pad = row_asg < 0
    pad_rank = jnp.cumsum(is_pad.astype(jnp.int32)) - 1
    row_src = jnp.where(is_pad, 0, row_asg % n_tok)
    row_dst = jnp.where(is_pad, n_asg + pad_rank, row_asg)
    block_exp = jnp.minimum(jnp.searchsorted(pend, jnp.arange(n_blocks, dtype=jnp.int32) * te, side='right'),
                            N_EXPERTS - 1).astype(jnp.int32)
    n_used = (pend[-1] // te).astype(jnp.int32).reshape(1)
    return block_exp, n_used, row_src.reshape(n_blocks, 1, te), row_dst.reshape(n_blocks, 1, te), n_rows


def _expert_kernel(te, n_blocks, bexp_ref, nused_ref, src_cur, src_next, dst_cur, h3_hbm, wg_ref, wu_ref, wd_ref,
                   y3_hbm, xbuf, ybuf, gsem, ssem, wgb, wub, wdb):
    i = pl.program_id(0)
    slot = i % 2
    tile = SUBLANES

    def gather_copy(tok, r, s):
        return pltpu.make_async_copy(h3_hbm.at[pl.ds(tok * tile, tile), :],
                                     xbuf.at[s, pl.ds(r * tile, tile), :], gsem.at[s])

    def scatter_copy(row, r, s):
        return pltpu.make_async_copy(ybuf.at[s, pl.ds(r * tile, tile), :],
                                     y3_hbm.at[pl.ds(row * tile, tile), :], ssem.at[s])

    def start_gather(src_ref, s):
        for r in range(te):
            gather_copy(src_ref[0, r], r, s).start(priority=r % 2)

    def wait_gather(s):
        for r in range(te):
            gather_copy(0, r, s).wait()

    def start_scatter(s):
        for r in range(te):
            scatter_copy(dst_cur[0, r], r, s).start(priority=r % 2)

    def wait_scatter(s):
        for r in range(te):
            scatter_copy(0, r, s).wait()

    n_used = nused_ref[0]

    @pl.when(i == 0)
    def _():
        start_gather(src_cur, 0)

    @pl.when(i >= n_used)
    def _():
        rows = te * tile
        fill = pltpu.make_async_copy(ybuf.at[0], y3_hbm.at[pl.ds(i * rows, rows), :], ssem.at[0])
        fill.start()
        fill.wait()

    @pl.when(i < n_used)
    def _():
        prev = bexp_ref[jnp.maximum(i - 1, 0)]

        @pl.when(jnp.logical_or(i == 0, bexp_ref[i] != prev))
        def _():
            wgb[...] = wg_ref[...].astype(BF16)
            wub[...] = wu_ref[...].astype(BF16)
            wdb[...] = wd_ref[...].astype(BF16)

        @pl.when(i >= 2)
        def _():
            wait_scatter(slot)

        wait_gather(slot)
        start_gather(src_next, 1 - slot)
        xb = _from_token_tiles(xbuf.at[slot], te).astype(BF16)
        hg = _dot(xb, wgb[...])
        hu = _dot(xb, wub[...])
        act = (_silu(hg) * hu).astype(BF16)
        _to_token_tiles(ybuf.at[slot], _dot(act, wdb[...]))
        start_scatter(slot)

        @pl.when(i == n_used - 1)
        def _():
            wait_gather(1 - slot)
            wait_scatter(slot)

            @pl.when(i >= 1)
            def _():
                wait_scatter(1 - slot)


def moe_experts(layer, h3, plan, n_tok, w_gate, w_up, w_down, te):
    block_exp, n_used, row_src, row_dst, n_rows = plan
    n_blocks = block_exp.shape[0]
    _, _, d, ff = w_gate.shape
    smem_blk = lambda f: pl.BlockSpec((None, 1, te), f, memory_space=pltpu.SMEM)
    grid_spec = pltpu.PrefetchScalarGridSpec(
        num_scalar_prefetch=2,
        grid=(n_blocks,),
        in_specs=[
            smem_blk(lambda i, be, nu: (i, 0, 0)),
            smem_blk(lambda i, be, nu: (jnp.minimum(i + 1, n_blocks - 1), 0, 0)),
            smem_blk(lambda i, be, nu: (i, 0, 0)),
            pl.BlockSpec(memory_space=pl.ANY),
            pl.BlockSpec((None, None, d, ff), lambda i, be, nu: (layer, be[i], 0, 0)),
            pl.BlockSpec((None, None, d, ff), lambda i, be, nu: (layer, be[i], 0, 0)),
            pl.BlockSpec((None, None, ff, d), lambda i, be, nu: (layer, be[i], 0, 0)),
        ],
        out_specs=pl.BlockSpec(memory_space=pl.ANY),
        scratch_shapes=[
            pltpu.VMEM((2, te * SUBLANES, LANES), F32),
            pltpu.VMEM((2, te * SUBLANES, LANES), F32),
            pltpu.SemaphoreType.DMA((2,)),
            pltpu.SemaphoreType.DMA((2,)),
            pltpu.VMEM((d, ff), BF16),
            pltpu.VMEM((d, ff), BF16),
            pltpu.VMEM((ff, d), BF16),
        ],
    )
    return pl.pallas_call(
        functools.partial(_expert_kernel, te, n_blocks),
        out_shape=jax.ShapeDtypeStruct((n_rows * SUBLANES, LANES), F32),
        grid_spec=grid_spec,
        compiler_params=_cparams(("arbitrary",)),
        name="moe_experts",
    )(block_exp, n_used, row_src, row_src, row_dst, h3, w_gate, w_up, w_down)


def _combine_kernel(tiles, final, x_ref, y0_ref, y1_ref, wt_ref, mod_ref, *rest):
    if final:
        fg_ref, o_ref = rest
    else:
        (o_ref,) = rest
    i = pl.program_id(0)
    tm, d = x_ref.shape
    gate = mod_ref[pl.ds(tiles.mod_row(i), 1), 5 * d:6 * d]
    wt = wt_ref[...]
    y = wt[:, 0:1] * _from_token_tiles(y0_ref, tm) + wt[:, 1:2] * _from_token_tiles(y1_ref, tm)
    out = x_ref[...] + gate * y
    if final:
        out = _rms(out) * fg_ref[...]
    o_ref[...] = out


def moe_combine(tiles, n_tiles, x, y3, wt, mods, final_gain):
    tm = tiles.tm
    d = x.shape[1]
    final = final_gain is not None
    in_specs = [
        pl.BlockSpec((tm, d), lambda i: (i, 0)),
        pl.BlockSpec((tm * SUBLANES, LANES), lambda i: (i, 0)),
        pl.BlockSpec((tm * SUBLANES, LANES), lambda i: (n_tiles + i, 0)),
        pl.BlockSpec((tm, LANES), lambda i: (i, 0)),
        pl.BlockSpec(mods.shape, lambda i: (0, 0)),
    ]
    args = [x, y3, y3, wt, mods]
    if final:
        in_specs.append(pl.BlockSpec((1, d), lambda i: (0, 0)))
        args.append(final_gain.reshape(1, d))
    return pl.pallas_call(
        functools.partial(_combine_kernel, tiles, final),
        out_shape=jax.ShapeDtypeStruct((n_tiles * tm, d), F32),
        grid=(n_tiles,),
        in_specs=in_specs,
        out_specs=pl.BlockSpec((tm, d), lambda i: (i, 0)),
        compiler_params=_cparams(("parallel",)),
        name="moe_combine",
    )(*args)


def moe_layer(tiles, layer, n_tiles, xs, gain, mods, grp_w, grp_b, exp_w, exp_b, w_gate, w_up, w_down, final_gain):
    n_tok = n_tiles * tiles.tm
    h3, eidx, wt = moe_router(tiles, n_tiles, xs, gain, mods, grp_w, grp_b, exp_w, exp_b)
    plan = _route_plan(eidx, n_tok, EXPERT_TILE)
    y3 = moe_experts(layer, h3, plan, n_tok, w_gate, w_up, w_down, EXPERT_TILE)
    return moe_combine(tiles, n_tiles, xs, y3, wt, mods, final_gain)


def kernel(x, c, ctx, c_ctx, mod_w, mod_b, norm_mix_g, norm_ffn_g, w_in_ab, conv_w, conv_b, conv_norm_g,
           w_out_ab, w_in_cd, q_norm_g, k_norm_g, cq_norm_g, ckv_norm_g, w_uq, w_ukv, w_out_cd,
           router_grp_w, router_grp_b, router_exp_w, router_exp_b, exp_w_gate, exp_w_up, exp_w_down,
           final_norm_g):
    n_batch, seq, d = x.shape
    ctx_len = ctx.shape[1]
    depth = mod_w.shape[0]
    tiles = _Tiles(n_batch, seq, ctx_len)
    xs = jnp.concatenate([x.reshape(n_batch * seq, d), ctx.reshape(n_batch * ctx_len, d)], axis=0)
    cvec = jnp.zeros((SUBLANES, d), F32).at[:n_batch].set(c).at[n_batch].set(c_ctx)
    mods = compute_mods(cvec, mod_w, mod_b)
    for layer in range(depth):
        last = layer == depth - 1
        j = layer // 2
        lmods = mods[layer]
        if layer % 2 == 0:
            pa, pug = norm_proj(xs, norm_mix_g[layer], lmods, w_in_ab[j].astype(BF16),
                                (A_WIDTH, 2 * B_WIDTH), 0, seq // 512, n_batch)
            ya_lat, ya_ctx = fourier_mix(pa, n_batch, seq, ctx_len, not last)
            xs = even_out(tiles, xs, ya_lat, ya_ctx, pug, conv_w[j], conv_b[j], conv_norm_g[j],
                          w_out_ab[j].astype(BF16), lmods)
        else:
            xs = attn_layer(tiles, xs, norm_mix_g[layer], lmods, w_in_cd[j].astype(BF16), q_norm_g[j], k_norm_g[j],
                            cq_norm_g[j], ckv_norm_g[j], _permute_w_uq(w_uq[j]).astype(BF16),
                            w_ukv[j].astype(BF16), w_out_cd[j].astype(BF16), not last)
        n_tiles = tiles.n_lat if last else tiles.n_tiles
        xs = moe_layer(tiles, layer, n_tiles, xs, norm_ffn_g[layer], lmods, router_grp_w[layer], router_grp_b[layer],
                       router_exp_w[layer], router_exp_b[layer], exp_w_gate, exp_w_up, exp_w_down,
                       final_norm_g if last else None)
    return xs.reshape(n_batch, seq, d)
```

```python
import functools
import math

import numpy as np
import jax
import jax.numpy as jnp
from jax import lax
from jax.experimental import pallas as pl
from jax.experimental.pallas import tpu as pltpu

F32 = jnp.float32
BF16 = jnp.bfloat16

D_MODEL = 1024
GRID_W = 64
A_GROUPS = 4
A_GROUP_DIM = 128
A_WIDTH = 512
B_WIDTH = 512
CONV_WIDTH = 31
CONV_PAD = 15
C_HEADS = 4
C_KV_HEADS = 2
C_HEAD_DIM = 128
C_WIDTH = 512
C_KV_WIDTH = 256
D_HEADS = 4
D_NOPE = 128
D_ROPE = 64
D_QK = D_NOPE + D_ROPE
D_V = 128
Q_LORA = 384
KV_LORA = 256
D_WIDTH = 512
N_GROUPS = 4
EXPERTS_PER_GROUP = 8
N_EXPERTS = 32
EXPERT_FF = 512
N_MOD = 6
RMS_EPS = 1e-6
ROPE_THETA = 10000.0
LOG2E = math.log2(math.e)

LANES = 128
SUBLANES = 8
ROW_TILE = 256
HALO = 16
FOURIER_N2 = 128
EXPERT_TILE = 256
VMEM_LIMIT = 56 * 1024 * 1024


def _cparams(sem):
    return pltpu.CompilerParams(dimension_semantics=sem, vmem_limit_bytes=VMEM_LIMIT)


def _split3(x):
    hi = x.astype(BF16)
    r1 = x - hi.astype(F32)
    mid = r1.astype(BF16)
    lo = (r1 - mid.astype(F32)).astype(BF16)
    return hi, mid, lo


def _dot(a, b):
    return jnp.dot(a, b, preferred_element_type=F32)


def _dot_nt(a, b):
    return lax.dot_general(a, b, (((1,), (1,)), ((), ())), preferred_element_type=F32)


def _dot_precise(a, b):
    a0, a1, a2 = _split3(a)
    b0, b1, b2 = _split3(b)
    out = _dot(a0, b0)
    out += _dot(a0, b1) + _dot(a1, b0)
    out += _dot(a0, b2) + _dot(a1, b1) + _dot(a2, b0)
    return out


def _dot_nt_precise(a, b):
    a0, a1, a2 = _split3(a)
    b0, b1, b2 = _split3(b)
    out = _dot_nt(a0, b0)
    out += _dot_nt(a0, b1) + _dot_nt(a1, b0)
    out += _dot_nt(a0, b2) + _dot_nt(a1, b1) + _dot_nt(a2, b0)
    return out


def _silu(x):
    return x * (1.0 / (1.0 + jnp.exp(-x)))


def _rms(x):
    return x * lax.rsqrt(jnp.mean(x * x, axis=-1, keepdims=True) + RMS_EPS)


def _mods_kernel(c_ref, w_ref, b_ref, o_ref):
    s = _silu(c_ref[...])
    o_ref[0] = _dot_precise(s, w_ref[0]) + b_ref[0]


def compute_mods(cvec, mod_w, mod_b):
    depth, d, n = mod_w.shape
    tn = 1024
    return pl.pallas_call(
        _mods_kernel,
        out_shape=jax.ShapeDtypeStruct((depth, SUBLANES, n), F32),
        grid=(depth, n // tn),
        in_specs=[
            pl.BlockSpec((SUBLANES, d), lambda l, j: (0, 0)),
            pl.BlockSpec((1, d, tn), lambda l, j: (l, 0, j)),
            pl.BlockSpec((1, 1, tn), lambda l, j: (l, 0, j)),
        ],
        out_specs=pl.BlockSpec((1, SUBLANES, tn), lambda l, j: (l, 0, j)),
        compiler_params=_cparams(("parallel", "parallel")),
        name="mods",
    )(cvec, mod_w, mod_b.reshape(depth, 1, n))


def _mod_row(i, tiles_per_batch, n_batch):
    return jnp.minimum(i // tiles_per_batch, n_batch)


def _norm_proj_kernel(splits, tiles_per_batch, n_batch, chunk, x_ref, g_ref, mod_ref, w_ref, *o_refs):
    i = pl.program_id(0)
    r = _mod_row(i, tiles_per_batch, n_batch)
    d = x_ref.shape[1]
    sh = mod_ref[pl.ds(r, 1), chunk * d:(chunk + 1) * d]
    sc = mod_ref[pl.ds(r, 1), (chunk + 1) * d:(chunk + 2) * d]
    h = _rms(x_ref[...]) * g_ref[...] * (1.0 + sc) + sh
    p = _dot(h.astype(BF16), w_ref[...])
    off = 0
    for o_ref, n in zip(o_refs, splits):
        o_ref[...] = p[:, off:off + n].astype(o_ref.dtype)
        off += n


def norm_proj(x, gain, mods, w_bf16, splits, chunk, tiles_per_batch, n_batch, tm=512):
    t, d = x.shape
    n = w_bf16.shape[1]
    assert sum(splits) == n and t % tm == 0
    kern = functools.partial(_norm_proj_kernel, tuple(splits), tiles_per_batch, n_batch, chunk)
    return pl.pallas_call(
        kern,
        out_shape=[jax.ShapeDtypeStruct((t, s), F32) for s in splits],
        grid=(t // tm,),
        in_specs=[
            pl.BlockSpec((tm, d), lambda i: (i, 0)),
            pl.BlockSpec((1, d), lambda i: (0, 0)),
            pl.BlockSpec(mods.shape, lambda i: (0, 0)),
            pl.BlockSpec((d, n), lambda i: (0, 0)),
        ],
        out_specs=[pl.BlockSpec((tm, s), lambda i: (i, 0)) for s in splits],
        compiler_params=_cparams(("parallel",)),
        name="norm_proj",
    )(x, gain.reshape(1, d), mods, w_bf16)


def _dft_tables(n1, n2, group_dim):
    ell = n1 * n2
    k1 = np.arange(n1, dtype=np.float64)
    ang1 = 2.0 * np.pi * np.outer(k1, k1) / n1
    f1 = np.concatenate([np.cos(ang1), -np.sin(ang1)], axis=0) / np.sqrt(n1)
    l2 = np.arange(n2, dtype=np.float64)
    k2 = np.arange(n2, dtype=np.float64)
    theta = 2.0 * np.pi * (k2[None, :, None] * l2[None, None, :] / n2
                           + k1[:, None, None] * l2[None, None, :] / ell)
    gr, gi = np.cos(theta), -np.sin(theta)
    g = np.concatenate([np.concatenate([gr, -gi], axis=2),
                        np.concatenate([gi, gr], axis=2)], axis=1) / np.sqrt(n2)
    c = np.arange(group_dim, dtype=np.float64)
    angc = 2.0 * np.pi * np.outer(c, c) / group_dim
    cs = np.concatenate([np.cos(angc), np.sin(angc)], axis=0) / np.sqrt(group_dim)
    return (jnp.asarray(f1, F32).astype(BF16), jnp.asarray(g, F32).astype(BF16),
            jnp.asarray(cs, F32).astype(BF16))


DFT_L2_BLOCK = 16


def _dft_stage1_kernel(f_ref, x_ref, o_ref):
    f = f_ref[...]
    for s in range(x_ref.shape[1]):
        o_ref[0, :, s, :] = _dot(f, x_ref[:, s, :].astype(BF16)).astype(o_ref.dtype)


def dft_stage1(f1, x3, n_batch, n1):
    _, n2, width = x3.shape
    sb = DFT_L2_BLOCK
    return pl.pallas_call(
        _dft_stage1_kernel,
        out_shape=jax.ShapeDtypeStruct((n_batch, 2 * n1, n2, width), BF16),
        grid=(n_batch, n2 // sb),
        in_specs=[
            pl.BlockSpec((2 * n1, n1), lambda b, j: (0, 0)),
            pl.BlockSpec((n1, sb, width), lambda b, j: (b, j, 0)),
        ],
        out_specs=pl.BlockSpec((1, 2 * n1, sb, width), lambda b, j: (b, 0, j, 0)),
        compiler_params=_cparams(("parallel", "parallel")),
        name="dft_stage1",
    )(f1, x3)


def _dft_stage2_kernel(kb, n2, real_input, *refs):
    if real_input:
        g_ref, cs_ref, yr_ref, o_ref = refs
    else:
        g_ref, cs_ref, yr_ref, yi_ref, o_ref = refs
    width = yr_ref.shape[-1]
    gd = cs_ref.shape[1]
    for k in range(kb):
        if real_input:
            z = _dot(g_ref[k, :, :n2], yr_ref[0, k].astype(BF16))
        else:
            y = jnp.concatenate([yr_ref[0, k].astype(BF16), yi_ref[0, k].astype(BF16)], axis=0)
            z = _dot(g_ref[k], y)
        zr, zi = z[:n2], z[n2:]
        for grp in range(width // gd):
            sl = slice(grp * gd, (grp + 1) * gd)
            lhs = jnp.concatenate([zr[:, sl], zi[:, sl]], axis=1).astype(BF16)
            o_ref[0, :, k, sl] = _dot(lhs, cs_ref[...]).astype(o_ref.dtype)


def dft_stage2(g, cs, y4, n1, n2, real_input, kb):
    n_batch, _, _, width = y4.shape
    kern = functools.partial(_dft_stage2_kernel, kb, n2, real_input)
    in_specs = [
        pl.BlockSpec((kb, 2 * n2, 2 * n2), lambda b, j: (j, 0, 0)),
        pl.BlockSpec(cs.shape, lambda b, j: (0, 0)),
        pl.BlockSpec((1, kb, n2, width), lambda b, j: (b, j, 0, 0)),
    ]
    args = [g, cs, y4]
    if not real_input:
        nj = n1 // kb
        in_specs.append(pl.BlockSpec((1, kb, n2, width), lambda b, j: (b, nj + j, 0, 0)))
        args.append(y4)
    return pl.pallas_call(
        kern,
        out_shape=jax.ShapeDtypeStruct((n_batch, n2, n1, width), BF16),
        grid=(n_batch, n1 // kb),
        in_specs=in_specs,
        out_specs=pl.BlockSpec((1, n2, kb, width), lambda b, j: (b, 0, j, 0)),
        compiler_params=_cparams(("parallel", "parallel")),
        name="dft_stage2",
    )(*args)


def fourier_mix(pa, n_batch, seq, ctx_len, with_ctx):
    t, width = pa.shape
    n2 = FOURIER_N2
    n1 = seq // n2
    f1, g, cs = _dft_tables(n1, n2, A_GROUP_DIM)
    y4 = dft_stage1(f1, pa.reshape(t // n2, n2, width), n_batch, n1)
    kb = min(16, n1)
    ya_lat = dft_stage2(g, cs, y4, n1, n2, False, kb).reshape(n_batch * seq, width)
    ya_ctx = None
    if with_ctx:
        _, gc, _ = _dft_tables(1, ctx_len, A_GROUP_DIM)
        xc = pa[n_batch * seq:].reshape(n_batch, 1, ctx_len, width)
        ya_ctx = dft_stage2(gc, cs, xc, 1, ctx_len, True, 1).reshape(n_batch * ctx_len, width)
    return ya_lat, ya_ctx


class _Tiles:
    def __init__(self, n_batch, seq, ctx_len, tm=ROW_TILE):
        assert seq % tm == 0 and ctx_len % tm == 0
        self.tm = tm
        self.n_batch = n_batch
        self.seq = seq
        self.ctx_len = ctx_len
        self.tps = seq // tm
        self.ctps = ctx_len // tm
        self.n_lat = n_batch * self.tps
        self.n_ctx = n_batch * self.ctps
        self.n_tiles = self.n_lat + self.n_ctx
        self.n_tok = self.n_tiles * tm

    def mod_row(self, i):
        return jnp.minimum(i // self.tps, self.n_batch)

    def lat_index(self, i):
        return jnp.minimum(i, self.n_lat - 1)

    def ctx_index(self, i):
        return jnp.maximum(i - self.n_lat, 0)

    def seq_pos(self, i):
        ic = i - self.n_lat
        is_ctx = i >= self.n_lat
        b = jnp.where(is_ctx, ic // self.ctps, i // self.tps)
        j = jnp.where(is_ctx, self.tps + ic % self.ctps, i % self.tps)
        return b, j

    def seq_first(self, i):
        ic = i - self.n_lat
        return jnp.where(i >= self.n_lat, ic % self.ctps == 0, i % self.tps == 0)

    def seq_last(self, i):
        ic = i - self.n_lat
        return jnp.where(i >= self.n_lat, ic % self.ctps == self.ctps - 1, i % self.tps == self.tps - 1)


CONV_CHUNK = 32


def _even_out_kernel(tiles, x_ref, yal_ref, yac_ref, ug_ref, prev_ref, next_ref, cw_ref, cb_ref, cg_ref,
                     w_ref, mod_ref, o_ref, ubuf, ybuf):
    i = pl.program_id(0)
    tm = tiles.tm
    bw = B_WIDTH
    d = x_ref.shape[1]

    def glu(ref):
        v = ref[...]
        return v[:, :bw] * (1.0 / (1.0 + jnp.exp(-v[:, bw:])))

    first = tiles.seq_first(i)
    last = tiles.seq_last(i)
    ubuf[0:HALO, :] = jnp.where(first, 0.0, glu(prev_ref))
    ubuf[HALO:HALO + tm, :] = glu(ug_ref)
    ubuf[HALO + tm:HALO + tm + HALO, :] = jnp.where(last, 0.0, glu(next_ref))

    ya = jnp.where(i >= tiles.n_lat, yac_ref[...], yal_ref[...])
    r = tiles.mod_row(i)
    gate = mod_ref[pl.ds(r, 1), 2 * d:3 * d]
    cb = cb_ref[...]
    cg = cg_ref[...]
    base = HALO - CONV_PAD
    for r0 in range(0, tm, CONV_CHUNK):
        acc = jnp.zeros((CONV_CHUNK, bw), F32)
        for k in range(CONV_WIDTH):
            acc = acc + cw_ref[k:k + 1, :] * ubuf[r0 + base + k:r0 + base + k + CONV_CHUNK, :]
        ybuf[r0:r0 + CONV_CHUNK, :] = _silu(_rms(acc + cb) * cg).astype(BF16)
    y = _dot(ya, w_ref[:A_WIDTH, :]) + _dot(ybuf[...], w_ref[A_WIDTH:, :])
    o_ref[...] = x_ref[...] + gate * y


def even_out(tiles, x, ya_lat, ya_ctx, pug, conv_w, conv_b, conv_g, w_out_bf16, mods):
    t, d = x.shape
    tm = tiles.tm
    n_tiles = t // tm
    hb = tm // HALO
    n_hblk = t // HALO
    if ya_ctx is None:
        ya_ctx = ya_lat
    cw = jnp.concatenate([conv_w, jnp.zeros((1, B_WIDTH), F32)], axis=0)
    kern = functools.partial(_even_out_kernel, tiles)
    return pl.pallas_call(
        kern,
        out_shape=jax.ShapeDtypeStruct((t, d), F32),
        grid=(n_tiles,),
        in_specs=[
            pl.BlockSpec((tm, d), lambda i: (i, 0)),
            pl.BlockSpec((tm, A_WIDTH), lambda i: (tiles.lat_index(i), 0)),
            pl.BlockSpec((tm, A_WIDTH), lambda i: (tiles.ctx_index(i), 0)),
            pl.BlockSpec((tm, 2 * B_WIDTH), lambda i: (i, 0)),
            pl.BlockSpec((HALO, 2 * B_WIDTH), lambda i: (jnp.maximum(i * hb - 1, 0), 0)),
            pl.BlockSpec((HALO, 2 * B_WIDTH), lambda i: (jnp.minimum((i + 1) * hb, n_hblk - 1), 0)),
            pl.BlockSpec((CONV_WIDTH + 1, B_WIDTH), lambda i: (0, 0)),
            pl.BlockSpec((1, B_WIDTH), lambda i: (0, 0)),
            pl.BlockSpec((1, B_WIDTH), lambda i: (0, 0)),
            pl.BlockSpec((d, d), lambda i: (0, 0)),
            pl.BlockSpec(mods.shape, lambda i: (0, 0)),
        ],
        out_specs=pl.BlockSpec((tm, d), lambda i: (i, 0)),
        scratch_shapes=[pltpu.VMEM((tm + 2 * HALO, B_WIDTH), F32), pltpu.VMEM((tm, B_WIDTH), BF16)],
        compiler_params=_cparams(("parallel",)),
        name="even_out",
    )(x, ya_lat, ya_ctx, pug, pug, pug, cw, conv_b.reshape(1, B_WIDTH), conv_g.reshape(1, B_WIDTH),
      w_out_bf16, mods)


def _rope_tables(seq, tm, dim):
    n_freq = dim // 4
    pos = jnp.arange(seq, dtype=jnp.int32)
    row = (pos // GRID_W).astype(F32)
    col = (pos % GRID_W).astype(F32)
    inv = ROPE_THETA ** (-jnp.arange(n_freq, dtype=F32) / n_freq)
    ang = jnp.concatenate([row[:, None] * inv[None, :], col[:, None] * inv[None, :]], axis=-1)
    cos = jnp.repeat(jnp.cos(ang), 2, axis=-1)
    sin = jnp.repeat(jnp.sin(ang), 2, axis=-1)
    sign = jnp.tile(jnp.asarray([-1.0, 1.0], F32), dim // 2)
    cos = jnp.concatenate([cos, jnp.ones((tm, dim), F32)], axis=0)
    sin = jnp.concatenate([sin * sign, jnp.zeros((tm, dim), F32)], axis=0)
    return cos, sin


def _pair_swap(x):
    n = x.shape[1]
    lane = lax.broadcasted_iota(jnp.int32, x.shape, 1)
    nxt = pltpu.roll(x, n - 1, 1)
    prv = pltpu.roll(x, 1, 1)
    return jnp.where((lane & 1) == 0, nxt, prv)


def _rope(x, cos, sin):
    parts = []
    for a in range(x.shape[1] // LANES):
        xa = x[:, a * LANES:(a + 1) * LANES]
        parts.append(xa * cos + _pair_swap(xa) * sin)
    return parts[0] if len(parts) == 1 else jnp.concatenate(parts, axis=1)


def _attn_prep_kernel(p_ref, cosc_ref, sinc_ref, cosd_ref, sind_ref, qg_ref, kg_ref, cqg_ref, ckvg_ref,
                      wuq_ref, wukv_ref, qc_ref, kc_ref, vct_ref, qd_ref, kd_ref, vdt_ref):
    hd = C_HEAD_DIM
    p = p_ref[...]
    o_k = C_WIDTH
    o_v = o_k + C_KV_WIDTH
    o_cq = o_v + C_KV_WIDTH
    o_ckv = o_cq + Q_LORA
    o_kr = o_ckv + KV_LORA
    cosc, sinc = cosc_ref[...], sinc_ref[...]
    for h in range(C_HEADS):
        qh = _rms(p[:, h * hd:(h + 1) * hd]) * qg_ref[...]
        qc_ref[h] = (_rope(qh, cosc, sinc) * (LOG2E * hd ** -0.5)).astype(qc_ref.dtype)
    for h in range(C_KV_HEADS):
        kh = _rms(p[:, o_k + h * hd:o_k + (h + 1) * hd]) * kg_ref[...]
        kc_ref[h] = _rope(kh, cosc, sinc).astype(kc_ref.dtype)
        vct_ref[h] = p[:, o_v + h * hd:o_v + (h + 1) * hd].T.astype(vct_ref.dtype)
    cosd = jnp.concatenate([cosd_ref[...], cosd_ref[...]], axis=1)
    sind = jnp.concatenate([sind_ref[...], sind_ref[...]], axis=1)
    cq = (_rms(p[:, o_cq:o_cq + Q_LORA]) * cqg_ref[...]).astype(BF16)
    qd = _dot(cq, wuq_ref[...]) * (LOG2E * D_QK ** -0.5)
    qr = _rope(qd[:, D_HEADS * D_NOPE:], cosd, sind)
    ckv = (_rms(p[:, o_ckv:o_ckv + KV_LORA]) * ckvg_ref[...]).astype(BF16)
    kvd = _dot(ckv, wukv_ref[...])
    kr2 = jnp.concatenate([p[:, o_kr:o_kr + D_ROPE], p[:, o_kr:o_kr + D_ROPE]], axis=1)
    kr = _rope(kr2, cosd, sind)[:, :D_ROPE]
    for h in range(D_HEADS):
        qd_ref[h, :, :D_NOPE] = qd[:, h * D_NOPE:(h + 1) * D_NOPE].astype(qd_ref.dtype)
        qd_ref[h, :, D_NOPE:] = qr[:, h * D_ROPE:(h + 1) * D_ROPE].astype(qd_ref.dtype)
        kd_ref[h, :, :D_NOPE] = kvd[:, h * 2 * D_NOPE:h * 2 * D_NOPE + D_NOPE].astype(kd_ref.dtype)
        kd_ref[h, :, D_NOPE:] = kr.astype(kd_ref.dtype)
        vdt_ref[h] = kvd[:, h * 2 * D_NOPE + D_NOPE:(h + 1) * 2 * D_NOPE].T.astype(vdt_ref.dtype)


def attn_prep(tiles, p, q_g, k_g, cq_g, ckv_g, w_uq_bf16, w_ukv_bf16):
    tm = tiles.tm
    nb = tiles.n_batch
    stot = tiles.seq + tiles.ctx_len
    cosc, sinc = _rope_tables(tiles.seq, tm, C_HEAD_DIM)
    cosd, sind = _rope_tables(tiles.seq, tm, D_ROPE)

    def rope_idx(i):
        return (jnp.where(i >= tiles.n_lat, tiles.tps, i % tiles.tps), 0)

    def rows_idx(i):
        b, j = tiles.seq_pos(i)
        return (b, 0, j, 0)

    def cols_idx(i):
        b, j = tiles.seq_pos(i)
        return (b, 0, 0, j)

    const = lambda i: (0, 0)
    out_shape = [
        jax.ShapeDtypeStruct((nb, C_HEADS, stot, C_HEAD_DIM), BF16),
        jax.ShapeDtypeStruct((nb, C_KV_HEADS, stot, C_HEAD_DIM), BF16),
        jax.ShapeDtypeStruct((nb, C_KV_HEADS, C_HEAD_DIM, stot), BF16),
        jax.ShapeDtypeStruct((nb, D_HEADS, stot, D_QK), BF16),
        jax.ShapeDtypeStruct((nb, D_HEADS, stot, D_QK), BF16),
        jax.ShapeDtypeStruct((nb, D_HEADS, D_V, stot), BF16),
    ]
    out_specs = [
        pl.BlockSpec((None, C_HEADS, tm, C_HEAD_DIM), rows_idx),
        pl.BlockSpec((None, C_KV_HEADS, tm, C_HEAD_DIM), rows_idx),
        pl.BlockSpec((None, C_KV_HEADS, C_HEAD_DIM, tm), cols_idx),
        pl.BlockSpec((None, D_HEADS, tm, D_QK), rows_idx),
        pl.BlockSpec((None, D_HEADS, tm, D_QK), rows_idx),
        pl.BlockSpec((None, D_HEADS, D_V, tm), cols_idx),
    ]
    return pl.pallas_call(
        _attn_prep_kernel,
        out_shape=out_shape,
        grid=(tiles.n_tiles,),
        in_specs=[
            pl.BlockSpec((tm, p.shape[1]), lambda i: (i, 0)),
            pl.BlockSpec((tm, C_HEAD_DIM), rope_idx),
            pl.BlockSpec((tm, C_HEAD_DIM), rope_idx),
            pl.BlockSpec((tm, D_ROPE), rope_idx),
            pl.BlockSpec((tm, D_ROPE), rope_idx),
            pl.BlockSpec((1, C_HEAD_DIM), const),
            pl.BlockSpec((1, C_HEAD_DIM), const),
            pl.BlockSpec((1, Q_LORA), const),
            pl.BlockSpec((1, KV_LORA), const),
            pl.BlockSpec(w_uq_bf16.shape, const),
            pl.BlockSpec(w_ukv_bf16.shape, const),
        ],
        out_specs=out_specs,
        compiler_params=_cparams(("parallel",)),
        name="attn_prep",
    )(p, cosc, sinc, cosd, sind, q_g.reshape(1, -1), k_g.reshape(1, -1), cq_g.reshape(1, -1),
      ckv_g.reshape(1, -1), w_uq_bf16, w_ukv_bf16)


def _permute_w_uq(w_uq):
    w = w_uq.reshape(Q_LORA, D_HEADS, D_QK)
    return jnp.concatenate([w[:, :, :D_NOPE].reshape(Q_LORA, D_HEADS * D_NOPE),
                            w[:, :, D_NOPE:].reshape(Q_LORA, D_HEADS * D_ROPE)], axis=1)


def _flash_kernel(sub, q_ref, k_ref, vt_ref, o_ref, m_ref, l_ref, acc_ref):
    ki = pl.program_id(3)
    n_k = pl.num_programs(3)
    tk = k_ref.shape[0]

    @pl.when(ki == 0)
    def _():
        m_ref[...] = jnp.full(m_ref.shape, -jnp.inf, F32)
        l_ref[...] = jnp.zeros(l_ref.shape, F32)
        acc_ref[...] = jnp.zeros(acc_ref.shape, F32)

    q = q_ref[...]
    m, l, acc = m_ref[...], l_ref[...], acc_ref[...]
    n_sub = tk // sub
    s_next = _dot_nt(k_ref[0:sub, :], q)
    for j in range(n_sub):
        s = s_next
        if j + 1 < n_sub:
            s_next = _dot_nt(k_ref[(j + 1) * sub:(j + 2) * sub, :], q)
        m_new = jnp.maximum(m, jnp.max(s, axis=0, keepdims=True))
        alpha = jnp.exp2(m - m_new)
        pr = jnp.exp2(s - m_new)
        l = alpha * l + jnp.sum(pr, axis=0, keepdims=True)
        acc = alpha * acc + _dot(vt_ref[:, j * sub:(j + 1) * sub], pr.astype(BF16))
        m = m_new
    m_ref[...], l_ref[...], acc_ref[...] = m, l, acc

    @pl.when(ki == n_k - 1)
    def _():
        o_ref[...] = (acc * (1.0 / l)).T.astype(o_ref.dtype)


def _pick_tile(n, cap, step=ROW_TILE):
    best = step
    for t in range(step, cap + 1, step):
        if n % t == 0:
            best = t
    return best


FLASH_TQ, FLASH_TK, FLASH_SUB = 512, 8448, 768


def flash_attention(q, k, vt, sq, sk, q_off, k_off, tq_cap=FLASH_TQ, tk_cap=FLASH_TK, sub=FLASH_SUB):
    nb, nh, _, dqk = q.shape
    nkh, dv = vt.shape[1], vt.shape[2]
    grp = nh // nkh
    tq = _pick_tile(math.gcd(sq, q_off) if q_off else sq, tq_cap)
    tk = _pick_tile(math.gcd(sk, k_off) if k_off else sk, tk_cap)
    qo, ko = q_off // tq, k_off // tk
    return pl.pallas_call(
        functools.partial(_flash_kernel, min(sub, tk)),
        out_shape=jax.ShapeDtypeStruct((nb, sq, nh * dv), BF16),
        grid=(nb, nh, sq // tq, sk // tk),
        in_specs=[
            pl.BlockSpec((None, None, tq, dqk), lambda b, h, qi, ki: (b, h, qo + qi, 0)),
            pl.BlockSpec((None, None, tk, dqk), lambda b, h, qi, ki: (b, h // grp, ko + ki, 0)),
            pl.BlockSpec((None, None, dv, tk), lambda b, h, qi, ki: (b, h // grp, 0, ko + ki)),
        ],
        out_specs=pl.BlockSpec((None, tq, dv), lambda b, h, qi, ki: (b, qi, h)),
        scratch_shapes=[pltpu.VMEM((1, tq), F32), pltpu.VMEM((1, tq), F32), pltpu.VMEM((dv, tq), F32)],
        compiler_params=_cparams(("parallel", "parallel", "parallel", "arbitrary")),
        name="flash_attention",
    )(q, k, vt)


def _attn_out_kernel(tiles, with_ctx, x_ref, *refs):
    if with_ctx:
        ocl_ref, odl_ref, occ_ref, odc_ref, w_ref, mod_ref, o_ref = refs
    else:
        ocl_ref, odl_ref, w_ref, mod_ref, o_ref = refs
    i = pl.program_id(0)
    d = x_ref.shape[1]
    oc, od = ocl_ref[...], odl_ref[...]
    if with_ctx:
        is_ctx = i >= tiles.n_lat
        oc = jnp.where(is_ctx, occ_ref[...], oc)
        od = jnp.where(is_ctx, odc_ref[...], od)
    gate = mod_ref[pl.ds(tiles.mod_row(i), 1), 2 * d:3 * d]
    y = _dot(oc, w_ref[:C_WIDTH, :]) + _dot(od, w_ref[C_WIDTH:, :])
    o_ref[...] = x_ref[...] + gate * y


def attn_out(tiles, x, oc_lat, od_lat, oc_ctx, od_ctx, w_out_bf16, mods):
    t, d = x.shape
    tm = tiles.tm
    with_ctx = oc_ctx is not None
    n_out_tiles = tiles.n_tiles if with_ctx else tiles.n_lat
    lat_spec = pl.BlockSpec((tm, C_WIDTH), lambda i: (tiles.lat_index(i), 0))
    ctx_spec = pl.BlockSpec((tm, C_WIDTH), lambda i: (tiles.ctx_index(i), 0))
    in_specs = [pl.BlockSpec((tm, d), lambda i: (i, 0)), lat_spec, lat_spec]
    args = [x, oc_lat, od_lat]
    if with_ctx:
        in_specs += [ctx_spec, ctx_spec]
        args += [oc_ctx, od_ctx]
    in_specs += [pl.BlockSpec((d, d), lambda i: (0, 0)), pl.BlockSpec(mods.shape, lambda i: (0, 0))]
    args += [w_out_bf16, mods]
    return pl.pallas_call(
        functools.partial(_attn_out_kernel, tiles, with_ctx),
        out_shape=jax.ShapeDtypeStruct((n_out_tiles * tm, d), F32),
        grid=(n_out_tiles,),
        in_specs=in_specs,
        out_specs=pl.BlockSpec((tm, d), lambda i: (i, 0)),
        compiler_params=_cparams(("parallel",)),
        name="attn_out",
    )(*args)


def attn_layer(tiles, xs, gain, mods, w_in_bf16, q_g, k_g, cq_g, ckv_g, w_uq_bf16, w_ukv_bf16, w_out_bf16, with_ctx_out):
    nb, seq, ctx_len = tiles.n_batch, tiles.seq, tiles.ctx_len
    (p,) = norm_proj(xs, gain, mods, w_in_bf16, (w_in_bf16.shape[1],), 0, seq // 512, nb)
    qc, kc, vct, qd, kd, vdt = attn_prep(tiles, p, q_g, k_g, cq_g, ckv_g, w_uq_bf16, w_ukv_bf16)
    stot = seq + ctx_len
    oc_lat = flash_attention(qc, kc, vct, seq, stot, 0, 0).reshape(nb * seq, C_WIDTH)
    od_lat = flash_attention(qd, kd, vdt, seq, stot, 0, 0).reshape(nb * seq, D_WIDTH)
    oc_ctx = od_ctx = None
    if with_ctx_out:
        oc_ctx = flash_attention(qc, kc, vct, ctx_len, ctx_len, seq, seq).reshape(nb * ctx_len, C_WIDTH)
        od_ctx = flash_attention(qd, kd, vdt, ctx_len, ctx_len, seq, seq).reshape(nb * ctx_len, D_WIDTH)
    return attn_out(tiles, xs, oc_lat, od_lat, oc_ctx, od_ctx, w_out_bf16, mods)


ROUTER_ROWS = 40
NEG_BIG = -1e30


def _to_token_tiles(ref, val):
    tm = val.shape[0]
    for s in range(SUBLANES):
        ref[pl.ds(s, tm, stride=SUBLANES), :] = val[:, s * LANES:(s + 1) * LANES]


def _from_token_tiles(ref, tm):
    return jnp.concatenate([ref[pl.ds(s, tm, stride=SUBLANES), :] for s in range(SUBLANES)], axis=1)


def _router_kernel(tiles, x_ref, g_ref, mod_ref, wr_ref, br_ref, tri_ref, xl_ref, cnt_ref, wt_ref):
    i = pl.program_id(0)
    d = x_ref.shape[1]
    tm = x_ref.shape[0]
    r = tiles.mod_row(i)
    sh = mod_ref[pl.ds(r, 1), 3 * d:4 * d]
    sc = mod_ref[pl.ds(r, 1), 4 * d:5 * d]
    h2 = _rms(x_ref[...]) * g_ref[...] * (1.0 + sc) + sh
    logits = _dot_nt_precise(wr_ref[...], h2) + br_ref[...]
    sub = lax.broadcasted_iota(jnp.int32, (SUBLANES, tm), 0)
    gl = logits[0:SUBLANES]
    gmax = jnp.max(gl, axis=0, keepdims=True)
    g_idx = jnp.min(jnp.where(gl == gmax, sub, SUBLANES), axis=0, keepdims=True)
    g_prob = 1.0 / jnp.sum(jnp.exp(gl - gmax), axis=0, keepdims=True)
    el = jnp.zeros((SUBLANES, tm), F32)
    for g in range(N_GROUPS):
        el = jnp.where(g_idx == g, logits[SUBLANES * (g + 1):SUBLANES * (g + 2)], el)
    v1 = jnp.max(el, axis=0, keepdims=True)
    i1 = jnp.min(jnp.where(el == v1, sub, SUBLANES), axis=0, keepdims=True)
    el2 = jnp.where(sub == i1, NEG_BIG, el)
    v2 = jnp.max(el2, axis=0, keepdims=True)
    i2 = jnp.min(jnp.where(el2 == v2, sub, SUBLANES), axis=0, keepdims=True)
    e2 = jnp.exp(v2 - v1)
    w1 = g_prob / (1.0 + e2)
    w2 = g_prob * e2 / (1.0 + e2)
    ea = jnp.concatenate([g_idx * EXPERTS_PER_GROUP + i1, g_idx * EXPERTS_PER_GROUP + i2], axis=1)
    erow = lax.broadcasted_iota(jnp.int32, (N_EXPERTS, 2 * tm), 0)
    onehot = (erow == ea).astype(F32)
    cnt = jnp.sum(onehot, axis=1, keepdims=True)
    cnt_b = jnp.broadcast_to(cnt, (N_EXPERTS, LANES))
    er = lax.broadcasted_iota(jnp.int32, (N_EXPERTS, N_EXPERTS), 0)
    ec = lax.broadcasted_iota(jnp.int32, (N_EXPERTS, N_EXPERTS), 1)
    base = _dot_precise((ec < er).astype(F32), cnt_b)[:, 0:1]
    prefix = _dot(onehot.astype(BF16), tri_ref[...])
    pos = jnp.sum(onehot * (base + prefix - 1.0), axis=0, keepdims=True)
    pos0, pos1 = pos[:, :tm], pos[:, tm:]
    prow = lax.broadcasted_iota(jnp.int32, (2 * tm, tm), 0).astype(F32)
    perm = jnp.logical_or(prow == pos0, prow == pos1).astype(BF16)
    _to_token_tiles(xl_ref, _dot(perm, h2.astype(BF16)))
    cnt_ref[0] = cnt_b.astype(jnp.int32)
    wrow = jnp.where(sub == 0, w1, jnp.where(sub == 1, w2, jnp.where(sub == 2, pos0, jnp.where(sub == 3, pos1, 0.0))))
    wfull = jnp.concatenate([wrow, jnp.zeros((LANES - SUBLANES, tm), F32)], axis=0)
    wt_ref[...] = wfull.T


def moe_router(tiles, n_tiles, x, gain, mods, grp_w, grp_b, exp_w, exp_b):
    d = x.shape[1]
    tm = tiles.tm
    n_tok = n_tiles * tm
    wr = jnp.concatenate([grp_w.T, jnp.zeros((SUBLANES - N_GROUPS, d), F32), exp_w.T], axis=0)
    br = jnp.concatenate([grp_b, jnp.full((SUBLANES - N_GROUPS,), NEG_BIG, F32), exp_b]).reshape(ROUTER_ROWS, 1)
    tri = jnp.asarray(np.triu(np.ones((2 * tm, 2 * tm), np.float32)), BF16)
    return pl.pallas_call(
        functools.partial(_router_kernel, tiles),
        out_shape=[jax.ShapeDtypeStruct((2 * n_tok * SUBLANES, LANES), F32),
                   jax.ShapeDtypeStruct((n_tiles, N_EXPERTS, LANES), jnp.int32),
                   jax.ShapeDtypeStruct((n_tok, LANES), F32)],
        grid=(n_tiles,),
        in_specs=[
            pl.BlockSpec((tm, d), lambda i: (i, 0)),
            pl.BlockSpec((1, d), lambda i: (0, 0)),
            pl.BlockSpec(mods.shape, lambda i: (0, 0)),
            pl.BlockSpec((ROUTER_ROWS, d), lambda i: (0, 0)),
            pl.BlockSpec((ROUTER_ROWS, 1), lambda i: (0, 0)),
            pl.BlockSpec((2 * tm, 2 * tm), lambda i: (0, 0)),
        ],
        out_specs=[pl.BlockSpec((2 * tm * SUBLANES, LANES), lambda i: (i, 0)),
                   pl.BlockSpec((1, N_EXPERTS, LANES), lambda i: (i, 0, 0)),
                   pl.BlockSpec((tm, LANES), lambda i: (i, 0))],
        compiler_params=_cparams(("parallel",)),
        name="moe_router",
    )(x, gain.reshape(1, d), mods, wr, br, tri)


def _route_plan(cnt, te):
    n_tiles = cnt.shape[0]
    n_asg = n_tiles * 2 * ROW_TILE
    n_blocks = (n_asg + N_EXPERTS * (te - 1)) // te
    i32 = jnp.int32
    tile_off = jnp.cumsum(cnt, axis=0) - cnt
    loc_off = jnp.cumsum(cnt, axis=1) - cnt
    counts = jnp.sum(cnt, axis=0)
    padded = (counts + te - 1) // te * te
    pend = jnp.cumsum(padded)
    pstart = pend - padded
    blk = jnp.arange(n_blocks, dtype=i32) * te
    block_exp = jnp.minimum(jnp.sum((pend[None, :] <= blk[:, None]).astype(i32), axis=1), N_EXPERTS - 1)
    n_used = (pend[-1] // te).reshape(1)
    k0 = blk - pstart[block_exp]
    off_be = tile_off.T[block_exp]
    end_be = off_be + cnt.T[block_exp]
    t_lo = jnp.sum((end_be <= k0[:, None]).astype(i32), axis=1)
    t_hi = jnp.sum((off_be < (k0 + te)[:, None]).astype(i32), axis=1)
    flat = lambda a: a.reshape(-1).astype(i32)
    return dict(block_exp=flat(block_exp), n_used=flat(n_used), k0=flat(k0), t_lo=flat(t_lo), t_hi=flat(t_hi),
                tile_off=flat(tile_off), cnt=flat(cnt), loc_off=flat(loc_off), pstart=flat(pstart),
                n_blocks=n_blocks)


def _expert_kernel(te, n_blocks, bexp_ref, nused_ref, src_cur, src_next, dst_cur, h3_hbm, wg_ref, wu_ref, wd_ref,
                   y3_hbm, xbuf, ybuf, gsem, ssem, wgb, wub, wdb):
    i = pl.program_id(0)
    slot = i % 2
    tile = SUBLANES

    def gather_copy(tok, r, s):
        return pltpu.make_async_copy(h3_hbm.at[pl.ds(tok * tile, tile), :],
                                     xbuf.at[s, pl.ds(r * tile, tile), :], gsem.at[s])

    def scatter_copy(row, r, s):
        return pltpu.make_async_copy(ybuf.at[s, pl.ds(r * tile, tile), :],
                                     y3_hbm.at[pl.ds(row * tile, tile), :], ssem.at[s])

    def start_gather(src_ref, s):
        for r in range(te):
            gather_copy(src_ref[0, r], r, s).start(priority=r % 2)

    def wait_gather(s):
        for r in range(te):
            gather_copy(0, r, s).wait()

    def start_scatter(s):
        for r in range(te):
            scatter_copy(dst_cur[0, r], r, s).start(priority=r % 2)

    def wait_scatter(s):
        for r in range(te):
            scatter_copy(0, r, s).wait()

    n_used = nused_ref[0]

    @pl.when(i == 0)
    def _():
        start_gather(src_cur, 0)

    @pl.when(i >= n_used)
    def _():
        rows = te * tile
        fill = pltpu.make_async_copy(ybuf.at[0], y3_hbm.at[pl.ds(i * rows, rows), :], ssem.at[0])
        fill.start()
        fill.wait()

    @pl.when(i < n_used)
    def _():
        prev = bexp_ref[jnp.maximum(i - 1, 0)]

        @pl.when(jnp.logical_or(i == 0, bexp_ref[i] != prev))
        def _():
            wgb[...] = wg_ref[...].astype(BF16)
            wub[...] = wu_ref[...].astype(BF16)
            wdb[...] = wd_ref[...].astype(BF16)

        @pl.when(i >= 2)
        def _():
            wait_scatter(slot)

        wait_gather(slot)
        start_gather(src_next, 1 - slot)
        xb = _from_token_tiles(xbuf.at[slot], te).astype(BF16)
        hg = _dot(xb, wgb[...])
        hu = _dot(xb, wub[...])
        act = (_silu(hg) * hu).astype(BF16)
        _to_token_tiles(ybuf.at[slot], _dot(act, wdb[...]))
        start_scatter(slot)

        @pl.when(i == n_used - 1)
        def _():
            wait_gather(1 - slot)
            wait_scatter(slot)

            @pl.when(i >= 1)
            def _():
                wait_scatter(1 - slot)


def moe_experts(layer, h3, plan, n_tok, w_gate, w_up, w_down, te):
    block_exp, n_used, row_src, row_dst, n_rows = plan
    n_blocks = block_exp.shape[0]
    _, _, d, ff = w_gate.shape
    smem_blk = lambda f: pl.BlockSpec((None, 1, te), f, memory_space=pltpu.SMEM)
    grid_spec = pltpu.PrefetchScalarGridSpec(
        num_scalar_prefetch=2,
        grid=(n_blocks,),
        in_specs=[
            smem_blk(lambda i, be, nu: (i, 0, 0)),
            smem_blk(lambda i, be, nu: (jnp.minimum(i + 1, n_blocks - 1), 0, 0)),
            smem_blk(lambda i, be, nu: (i, 0, 0)),
            pl.BlockSpec(memory_space=pl.ANY),
            pl.BlockSpec((None, None, d, ff), lambda i, be, nu: (layer, be[i], 0, 0)),
            pl.BlockSpec((None, None, d, ff), lambda i, be, nu: (layer, be[i], 0, 0)),
            pl.BlockSpec((None, None, ff, d), lambda i, be, nu: (layer, be[i], 0, 0)),
        ],
        out_specs=pl.BlockSpec(memory_space=pl.ANY),
        scratch_shapes=[
            pltpu.VMEM((2, te * SUBLANES, LANES), F32),
            pltpu.VMEM((2, te * SUBLANES, LANES), F32),
            pltpu.SemaphoreType.DMA((2,)),
            pltpu.SemaphoreType.DMA((2,)),
            pltpu.VMEM((d, ff), BF16),
            pltpu.VMEM((d, ff), BF16),
            pltpu.VMEM((ff, d), BF16),
        ],
    )
    return pl.pallas_call(
        functools.partial(_expert_kernel, te, n_blocks),
        out_shape=jax.ShapeDtypeStruct((n_rows * SUBLANES, LANES), F32),
        grid_spec=grid_spec,
        compiler_params=_cparams(("arbitrary",)),
        name="moe_experts",
    )(block_exp, n_used, row_src, row_src, row_dst, h3, w_gate, w_up, w_down)


def _combine_kernel(tiles, final, x_ref, y0_ref, y1_ref, wt_ref, mod_ref, *rest):
    if final:
        fg_ref, o_ref = rest
    else:
        (o_ref,) = rest
    i = pl.program_id(0)
    tm, d = x_ref.shape
    gate = mod_ref[pl.ds(tiles.mod_row(i), 1), 5 * d:6 * d]
    wt = wt_ref[...]
    y = wt[:, 0:1] * _from_token_tiles(y0_ref, tm) + wt[:, 1:2] * _from_token_tiles(y1_ref, tm)
    out = x_ref[...] + gate * y
    if final:
        out = _rms(out) * fg_ref[...]
    o_ref[...] = out


def moe_combine(tiles, n_tiles, x, y3, wt, mods, final_gain):
    tm = tiles.tm
    d = x.shape[1]
    final = final_gain is not None
    in_specs = [
        pl.BlockSpec((tm, d), lambda i: (i, 0)),
        pl.BlockSpec((tm * SUBLANES, LANES), lambda i: (i, 0)),
        pl.BlockSpec((tm * SUBLANES, LANES), lambda i: (n_tiles + i, 0)),
        pl.BlockSpec((tm, LANES), lambda i: (i, 0)),
        pl.BlockSpec(mods.shape, lambda i: (0, 0)),
    ]
    args = [x, y3, y3, wt, mods]
    if final:
        in_specs.append(pl.BlockSpec((1, d), lambda i: (0, 0)))
        args.append(final_gain.reshape(1, d))
    return pl.pallas_call(
        functools.partial(_combine_kernel, tiles, final),
        out_shape=jax.ShapeDtypeStruct((n_tiles * tm, d), F32),
        grid=(n_tiles,),
        in_specs=in_specs,
        out_specs=pl.BlockSpec((tm, d), lambda i: (i, 0)),
        compiler_params=_cparams(("parallel",)),
        name="moe_combine",
    )(*args)


def moe_layer(tiles, layer, n_tiles, xs, gain, mods, grp_w, grp_b, exp_w, exp_b, w_gate, w_up, w_down, final_gain):
    n_tok = n_tiles * tiles.tm
    h3, eidx, wt = moe_router(tiles, n_tiles, xs, gain, mods, grp_w, grp_b, exp_w, exp_b)
    plan = _route_plan(eidx, n_tok, EXPERT_TILE)
    y3 = moe_experts(layer, h3, plan, n_tok, w_gate, w_up, w_down, EXPERT_TILE)
    return moe_combine(tiles, n_tiles, xs, y3, wt, mods, final_gain)


def _expert_kernel(te, tm2, bexp_ref, nused_ref, k0_ref, tlo_ref, thi_ref, toff_ref, cnt_ref, loff_ref,
                   xl_hbm, wg_ref, wu_ref, wd_ref, y_ref, xbuf, gsem, wgb, wub, wdb):
    i = pl.program_id(0)
    slot = i % 2
    n_used = nused_ref[0]
    tile = SUBLANES

    def run_copies(j, s, fn):
        e = bexp_ref[j]
        kk = k0_ref[j]

        def body(t, c):
            off = toff_ref[t * N_EXPERTS + e]
            lo = jnp.maximum(off, kk)
            hi = jnp.minimum(off + cnt_ref[t * N_EXPERTS + e], kk + te)
            n = hi - lo

            @pl.when(n > 0)
            def _():
                src = (t * tm2 + loff_ref[t * N_EXPERTS + e] + lo - off) * tile
                fn(pltpu.make_async_copy(xl_hbm.at[pl.ds(src, n * tile), :],
                                         xbuf.at[s, pl.ds((lo - kk) * tile, n * tile), :], gsem.at[s]))
            return c

        lax.fori_loop(tlo_ref[j], thi_ref[j], body, 0)

    @pl.when(i == 0)
    def _():
        xbuf[...] = jnp.zeros(xbuf.shape, F32)
        run_copies(0, 0, lambda cp: cp.start())

    @pl.when(i + 1 < n_used)
    def _():
        run_copies(i + 1, 1 - slot, lambda cp: cp.start())

    @pl.when(i >= n_used)
    def _():
        y_ref[...] = jnp.zeros(y_ref.shape, F32)

    @pl.when(i < n_used)
    def _():
        run_copies(i, slot, lambda cp: cp.wait())
        prev = bexp_ref[jnp.maximum(i - 1, 0)]

        @pl.when(jnp.logical_or(i == 0, bexp_ref[i] != prev))
        def _():
            wgb[...] = wg_ref[...].astype(BF16)
            wub[...] = wu_ref[...].astype(BF16)
            wdb[...] = wd_ref[...].astype(BF16)

        xb = _from_token_tiles(xbuf.at[slot], te).astype(BF16)
        hg = _dot(xb, wgb[...])
        hu = _dot(xb, wub[...])
        act = (_silu(hg) * hu).astype(BF16)
        _to_token_tiles(y_ref, _dot(act, wdb[...]))


def moe_experts(layer, xl, plan, w_gate, w_up, w_down, te):
    n_blocks = plan["n_blocks"]
    _, _, d, ff = w_gate.shape
    wmap = lambda i, be, *_: (layer, be[i], 0, 0)
    grid_spec = pltpu.PrefetchScalarGridSpec(
        num_scalar_prefetch=8,
        grid=(n_blocks,),
        in_specs=[
            pl.BlockSpec(memory_space=pl.ANY),
            pl.BlockSpec((None, None, d, ff), wmap),
            pl.BlockSpec((None, None, d, ff), wmap),
            pl.BlockSpec((None, None, ff, d), wmap),
        ],
        out_specs=pl.BlockSpec((te * SUBLANES, LANES), lambda i, *_: (i, 0)),
        scratch_shapes=[
            pltpu.VMEM((2, te * SUBLANES, LANES), F32),
            pltpu.SemaphoreType.DMA((2,)),
            pltpu.VMEM((d, ff), BF16),
            pltpu.VMEM((d, ff), BF16),
            pltpu.VMEM((ff, d), BF16),
        ],
    )
    return pl.pallas_call(
        functools.partial(_expert_kernel, te, 2 * ROW_TILE),
        out_shape=jax.ShapeDtypeStruct((n_blocks * te * SUBLANES, LANES), F32),
        grid_spec=grid_spec,
        compiler_params=_cparams(("arbitrary",)),
        name="moe_experts",
    )(plan["block_exp"], plan["n_used"], plan["k0"], plan["t_lo"], plan["t_hi"], plan["tile_off"], plan["cnt"],
      plan["loc_off"], xl, w_gate, w_up, w_down)


def _combine_kernel(tiles, final, n_tiles, toff_ref, cnt_ref, loff_ref, pstart_ref, x_ref, ys_hbm, wt_ref, mod_ref,
                    *rest):
    if final:
        fg_ref, o_ref, ybuf, sem = rest
    else:
        o_ref, ybuf, sem = rest
    i = pl.program_id(0)
    slot = i % 2
    tm, d = x_ref.shape
    tile = SUBLANES

    def run_copies(t, s, fn):
        for e in range(N_EXPERTS):
            n = cnt_ref[t * N_EXPERTS + e]

            @pl.when(n > 0)
            def _():
                src = (pstart_ref[e] + toff_ref[t * N_EXPERTS + e]) * tile
                fn(pltpu.make_async_copy(ys_hbm.at[pl.ds(src, n * tile), :],
                                         ybuf.at[s, pl.ds(loff_ref[t * N_EXPERTS + e] * tile, n * tile), :],
                                         sem.at[s]))

    @pl.when(i == 0)
    def _():
        run_copies(0, 0, lambda cp: cp.start())

    @pl.when(i + 1 < n_tiles)
    def _():
        run_copies(i + 1, 1 - slot, lambda cp: cp.start())

    run_copies(i, slot, lambda cp: cp.wait())
    yl = _from_token_tiles(ybuf.at[slot], 2 * tm).astype(BF16)
    wt = wt_ref[...]
    lane = lax.broadcasted_iota(jnp.int32, (tm, 2 * tm), 1).astype(F32)
    sel0 = (lane == wt[:, 2:3]).astype(BF16)
    sel1 = (lane == wt[:, 3:4]).astype(BF16)
    y = wt[:, 0:1] * _dot(sel0, yl) + wt[:, 1:2] * _dot(sel1, yl)
    gate = mod_ref[pl.ds(tiles.mod_row(i), 1), 5 * d:6 * d]
    out = x_ref[...] + gate * y
    if final:
        out = _rms(out) * fg_ref[...]
    o_ref[...] = out


def moe_combine(tiles, n_tiles, x, ys, wt, plan, mods, final_gain):
    tm = tiles.tm
    d = x.shape[1]
    final = final_gain is not None
    in_specs = [
        pl.BlockSpec((tm, d), lambda i, *_: (i, 0)),
        pl.BlockSpec(memory_space=pl.ANY),
        pl.BlockSpec((tm, LANES), lambda i, *_: (i, 0)),
        pl.BlockSpec(mods.shape, lambda i, *_: (0, 0)),
    ]
    args = [x, ys, wt, mods]
    if final:
        in_specs.append(pl.BlockSpec((1, d), lambda i, *_: (0, 0)))
        args.append(final_gain.reshape(1, d))
    grid_spec = pltpu.PrefetchScalarGridSpec(
        num_scalar_prefetch=4,
        grid=(n_tiles,),
        in_specs=in_specs,
        out_specs=pl.BlockSpec((tm, d), lambda i, *_: (i, 0)),
        scratch_shapes=[pltpu.VMEM((2, 2 * tm * SUBLANES, LANES), F32), pltpu.SemaphoreType.DMA((2,))],
    )
    return pl.pallas_call(
        functools.partial(_combine_kernel, tiles, final, n_tiles),
        out_shape=jax.ShapeDtypeStruct((n_tiles * tm, d), F32),
        grid_spec=grid_spec,
        compiler_params=_cparams(("arbitrary",)),
        name="moe_combine",
    )(plan["tile_off"], plan["cnt"], plan["loc_off"], plan["pstart"], *args)


def moe_layer(tiles, layer, n_tiles, xs, gain, mods, grp_w, grp_b, exp_w, exp_b, w_gate, w_up, w_down, final_gain):
    xl, cnt, wt = moe_router(tiles, n_tiles, xs, gain, mods, grp_w, grp_b, exp_w, exp_b)
    plan = _route_plan(cnt[:, :, 0], EXPERT_TILE)
    ys = moe_experts(layer, xl, plan, w_gate, w_up, w_down, EXPERT_TILE)
    return moe_combine(tiles, n_tiles, xs, ys, wt, plan, mods, final_gain)


def kernel(x, c, ctx, c_ctx, mod_w, mod_b, norm_mix_g, norm_ffn_g, w_in_ab, conv_w, conv_b, conv_norm_g,
           w_out_ab, w_in_cd, q_norm_g, k_norm_g, cq_norm_g, ckv_norm_g, w_uq, w_ukv, w_out_cd,
           router_grp_w, router_grp_b, router_exp_w, router_exp_b, exp_w_gate, exp_w_up, exp_w_down,
           final_norm_g):
    n_batch, seq, d = x.shape
    ctx_len = ctx.shape[1]
    depth = mod_w.shape[0]
    tiles = _Tiles(n_batch, seq, ctx_len)
    xs = jnp.concatenate([x.reshape(n_batch * seq, d), ctx.reshape(n_batch * ctx_len, d)], axis=0)
    cvec = jnp.zeros((SUBLANES, d), F32).at[:n_batch].set(c).at[n_batch].set(c_ctx)
    mods = compute_mods(cvec, mod_w, mod_b)
    for layer in range(depth):
        last = layer == depth - 1
        j = layer // 2
        lmods = mods[layer]
        if layer % 2 == 0:
            pa, pug = norm_proj(xs, norm_mix_g[layer], lmods, w_in_ab[j].astype(BF16),
                                (A_WIDTH, 2 * B_WIDTH), 0, seq // 512, n_batch)
            ya_lat, ya_ctx = fourier_mix(pa, n_batch, seq, ctx_len, not last)
            xs = even_out(tiles, xs, ya_lat, ya_ctx, pug, conv_w[j], conv_b[j], conv_norm_g[j],
                          w_out_ab[j].astype(BF16), lmods)
        else:
            xs = attn_layer(tiles, xs, norm_mix_g[layer], lmods, w_in_cd[j].astype(BF16), q_norm_g[j], k_norm_g[j],
                            cq_norm_g[j], ckv_norm_g[j], _permute_w_uq(w_uq[j]).astype(BF16),
                            w_ukv[j].astype(BF16), w_out_cd[j].astype(BF16), not last)
        n_tiles = tiles.n_lat if last else tiles.n_tiles
        xs = moe_layer(tiles, layer, n_tiles, xs, norm_ffn_g[layer], lmods, router_grp_w[layer], router_grp_b[layer],
                       router_exp_w[layer], router_exp_b[layer], exp_w_gate, exp_w_up, exp_w_down,
                       final_norm_g if last else None)
    return xs.reshape(n_batch, seq, d)
```
